```python
import math
import jax
import jax.numpy as jnp
from jax import lax
import numpy as np

D_MODEL = 2048
BATCH = 16
SEQ = 2048
DEPTH = 4

MEM_LEN = 256
EPS = 1e-6
ROPE_THETA = 10000.0
POOL_WINDOWS = (2, 4, 8, 16)
POOL_GROUPS = len(POOL_WINDOWS)
POOL_GC = D_MODEL // POOL_GROUPS
N_HEADS = 16
HEAD_DIM = D_MODEL // N_HEADS
IDX_HEADS = 16
IDX_DIM = 64
TOPK_MAX = 256
Q_BLOCK = 128
X_HEADS = 4
X_HEAD_DIM = 128
FFN_HIDDEN = ((8 * D_MODEL + 3 * 256 - 1) // (3 * 256)) * 256
ATTN_SPLITS = (N_HEADS * HEAD_DIM, HEAD_DIM, HEAD_DIM, IDX_HEADS * IDX_DIM, IDX_DIM, IDX_HEADS)
ATTN_IN = sum(ATTN_SPLITS)
N_POOL_LAYERS = (DEPTH + 1) // 2
N_ATTN_LAYERS = DEPTH // 2

kernel_name = 'hybrid_pool_dsa_memxattn_trunk'


def rms_norm(x, g):
    xf = x.astype(jnp.float32)
    y = xf * lax.rsqrt(jnp.mean(xf * xf, axis=-1, keepdims=True) + EPS)
    return (y * g.astype(jnp.float32)).astype(x.dtype)


def rope_tables(positions, dim):
    inv_freq = ROPE_THETA ** (-(jnp.arange(0, dim, 2, dtype=jnp.float32) / dim))
    ang = positions.astype(jnp.float32)[..., None] * inv_freq
    return jnp.cos(ang), jnp.sin(ang)


def apply_rope(x, cos, sin):
    xf = x.astype(jnp.float32)
    x1, x2 = jnp.split(xf, 2, axis=-1)
    out = jnp.concatenate([x1 * cos - x2 * sin, x2 * cos + x1 * sin], axis=-1)
    return out.astype(x.dtype)


def pool_mixer(h, w, scale):
    B, S, D = h.shape
    hg = h.reshape(B, S, POOL_GROUPS, POOL_GC).astype(jnp.float32)
    c = jnp.concatenate([jnp.zeros((B, 1, POOL_GROUPS, POOL_GC), jnp.float32),
                         jnp.cumsum(hg, axis=1)], axis=1)
    t = jnp.arange(S)
    means = []
    for g, win in enumerate(POOL_WINDOWS):
        lo = jnp.maximum(t + 1 - win, 0)
        cnt = (t + 1 - lo).astype(jnp.float32)
        means.append((c[:, 1:, g] - c[:, lo, g]) / cnt[None, :, None])
    diff = (jnp.stack(means, axis=2) - hg).astype(h.dtype)
    y = jnp.einsum('bsgc,gcd->bsgd', diff, w).reshape(B, S, D)
    return y * scale


def dsa_attention(h, w_in, w_out, cos_a, sin_a, cos_i, sin_i):
    B, S, _ = h.shape
    proj = h @ w_in
    bounds = list(np.cumsum(ATTN_SPLITS)[:-1])
    q, k, v, qi, ki, wi = jnp.split(proj, bounds, axis=-1)
    q = apply_rope(q.reshape(B, S, N_HEADS, HEAD_DIM), cos_a[:, :, None, :], sin_a[:, :, None, :])
    k = apply_rope(k, cos_a, sin_a)
    qi = apply_rope(qi.reshape(B, S, IDX_HEADS, IDX_DIM), cos_i[:, :, None, :], sin_i[:, :, None, :])
    ki = apply_rope(ki, cos_i, sin_i)
    wi = wi * (IDX_HEADS ** -0.5 * IDX_DIM ** -0.5)
    topk = min(TOPK_MAX, S // 4)
    n_blocks = S // Q_BLOCK
    key_pos = jnp.arange(S)
    att_scale = HEAD_DIM ** -0.5
    gather = jax.vmap(lambda src, ids: src[ids])

    def block(start):
        q_b = lax.dynamic_slice_in_dim(q, start, Q_BLOCK, axis=1)
        qi_b = lax.dynamic_slice_in_dim(qi, start, Q_BLOCK, axis=1)
        wi_b = lax.dynamic_slice_in_dim(wi, start, Q_BLOCK, axis=1)
        q_pos = start + jnp.arange(Q_BLOCK)
        causal = key_pos[None, :] <= q_pos[:, None]
        rel = jax.nn.relu(jnp.einsum('bqhd,bsd->bqhs', qi_b, ki))
        iscore = jnp.einsum('bqh,bqhs->bqs', wi_b, rel).astype(jnp.float32)
        iscore = jnp.where(causal[None], iscore, -jnp.inf)
        _, idx = lax.top_k(iscore, topk)
        valid = idx <= q_pos[None, :, None]
        k_sel = gather(k, idx)
        v_sel = gather(v, idx)
        logits = jnp.einsum('bqhd,bqkd->bqhk', q_b, k_sel).astype(jnp.float32) * att_scale
        logits = jnp.where(valid[:, :, None, :], logits, -jnp.inf)
        p = jax.nn.softmax(logits, axis=-1).astype(v_sel.dtype)
        return jnp.einsum('bqhk,bqkd->bqhd', p, v_sel)

    out = lax.map(block, jnp.arange(n_blocks) * Q_BLOCK)
    out = jnp.moveaxis(out, 0, 1).reshape(B, S, N_HEADS * HEAD_DIM)
    return out @ w_out


def memory_cross_attention(h, memn, w_q, w_kv, w_o):
    B, S, _ = h.shape
    M = memn.shape[1]
    q = (h @ w_q).reshape(B, S, X_HEADS, X_HEAD_DIM)
    kv = (memn @ w_kv).reshape(B, M, 2, X_HEADS, X_HEAD_DIM)
    k, v = kv[:, :, 0], kv[:, :, 1]
    logits = jnp.einsum('bshd,bmhd->bhsm', q, k).astype(jnp.float32) * (X_HEAD_DIM ** -0.5)
    p = jax.nn.softmax(logits, axis=-1).astype(v.dtype)
    out = jnp.einsum('bhsm,bmhd->bshd', p, v).reshape(B, S, X_HEADS * X_HEAD_DIM)
    return out @ w_o


def swiglu(h, w_in, w_out):
    g, u = jnp.split(h @ w_in, 2, axis=-1)
    return (jax.nn.silu(g) * u) @ w_out


def setup_inputs(seed: int = 0) -> dict:
    key = jax.random.key(seed)
    ks = jax.random.split(key, 17)

    def nrm(k, shape, fan_in):
        return jax.random.normal(k, shape, jnp.float32) * (fan_in ** -0.5)

    def gain(k, shape):
        return 1.0 + 0.02 * jax.random.normal(k, shape, jnp.float32)

    x = jax.random.normal(ks[0], (BATCH, SEQ, D_MODEL), jnp.float32)
    mem = jax.random.normal(ks[1], (BATCH, MEM_LEN, D_MODEL), jnp.float32)
    positions = (jax.random.randint(ks[2], (BATCH, 1), 0, 1024, dtype=jnp.int32)
                 + jnp.arange(SEQ, dtype=jnp.int32)[None, :])
    return {
        'x': x,
        'mem': mem,
        'positions': positions,
        'norm_mix': gain(ks[3], (DEPTH, D_MODEL)),
        'norm_xattn': gain(ks[4], (DEPTH, D_MODEL)),
        'norm_ffn': gain(ks[5], (DEPTH, D_MODEL)),
        'norm_memory': gain(ks[6], (D_MODEL,)),
        'norm_final': gain(ks[7], (D_MODEL,)),
        'pool_w': nrm(ks[8], (N_POOL_LAYERS, POOL_GROUPS, POOL_GC, POOL_GC), POOL_GC),
        'pool_scale': 1.0 + 0.1 * jax.random.normal(ks[9], (N_POOL_LAYERS, D_MODEL), jnp.float32),
        'attn_w_in': nrm(ks[10], (N_ATTN_LAYERS, D_MODEL, ATTN_IN), D_MODEL),
        'attn_w_out': nrm(ks[11], (N_ATTN_LAYERS, N_HEADS * HEAD_DIM, D_MODEL), N_HEADS * HEAD_DIM),
        'xattn_w_q': nrm(ks[12], (DEPTH, D_MODEL, X_HEADS * X_HEAD_DIM), D_MODEL),
        'xattn_w_kv': nrm(ks[13], (DEPTH, D_MODEL, 2 * X_HEADS * X_HEAD_DIM), D_MODEL),
        'xattn_w_o': nrm(ks[14], (DEPTH, X_HEADS * X_HEAD_DIM, D_MODEL), X_HEADS * X_HEAD_DIM),
        'ffn_w_in': nrm(ks[15], (DEPTH, D_MODEL, 2 * FFN_HIDDEN), D_MODEL),
        'ffn_w_out': nrm(ks[16], (DEPTH, FFN_HIDDEN, D_MODEL), FFN_HIDDEN),
    }


def reference(x, mem, positions, norm_mix, norm_xattn, norm_ffn, norm_memory, norm_final,
              pool_w, pool_scale, attn_w_in, attn_w_out, xattn_w_q, xattn_w_kv, xattn_w_o,
              ffn_w_in, ffn_w_out):
    cos_a, sin_a = rope_tables(positions, HEAD_DIM)
    cos_i, sin_i = rope_tables(positions, IDX_DIM)
    memn = rms_norm(mem, norm_memory)
    ia = 0
    ib = 0
    for i in range(DEPTH):
        h = rms_norm(x, norm_mix[i])
        if i % 2 == 0:
            x = x + pool_mixer(h, pool_w[ia], pool_scale[ia])
            ia += 1
        else:
            x = x + dsa_attention(h, attn_w_in[ib], attn_w_out[ib], cos_a, sin_a, cos_i, sin_i)
            ib += 1
        x = x + memory_cross_attention(rms_norm(x, norm_xattn[i]), memn,
                                       xattn_w_q[i], xattn_w_kv[i], xattn_w_o[i])
        x = x + swiglu(rms_norm(x, norm_ffn[i]), ffn_w_in[i], ffn_w_out[i])
    return rms_norm(x, norm_final)
```

```python
import functools

import jax
import jax.numpy as jnp
from jax import lax
from jax.experimental import pallas as pl
from jax.experimental.pallas import tpu as pltpu

EPS = 1e-6
ROPE_THETA = 10000.0
POOL_WINDOWS = (2, 4, 8, 16)
POOL_HALO = 16
HEAD_DIM = 128
IDX_HEADS = 16
IDX_DIM = 64
TOPK_MAX = 256
X_HEADS = 4
X_HEAD_DIM = 128

BF16 = jnp.bfloat16
F32 = jnp.float32
INT_MIN = -(2**31)
MASK_BIAS = -1e30
VMEM_LIMIT_BYTES = 56 * 1024 * 1024

_NT = (((1,), (1,)), ((), ()))


def _cparams(*sem):
    return pltpu.CompilerParams(dimension_semantics=sem, vmem_limit_bytes=VMEM_LIMIT_BYTES)


def _rms(x, g):
    ms = jnp.mean(x * x, axis=-1, keepdims=True)
    return x * lax.rsqrt(ms + EPS) * g


def _dot(a, b):
    return jnp.dot(a, b, preferred_element_type=F32)


def _dot_nt(a, b):
    return lax.dot_general(a, b, _NT, preferred_element_type=F32)


def _const_spec(shape):
    n = len(shape)
    return pl.BlockSpec(shape, lambda *_: (0,) * n, pipeline_mode=pl.Buffered(1))


def _memkv_kernel(mem_ref, g_ref, w_ref, o_ref):
    hn = _rms(mem_ref[...], g_ref[...]).astype(BF16)
    o_ref[0] = _dot(hn, w_ref[0]).astype(BF16)


def _memkv(mem2d, g, w_kv):
    rows, d = mem2d.shape
    depth, _, n = w_kv.shape
    tr = min(512, rows)
    return pl.pallas_call(
        _memkv_kernel,
        grid=(rows // tr, depth),
        in_specs=[
            pl.BlockSpec((tr, d), lambda i, l: (i, 0)),
            pl.BlockSpec((1, d), lambda i, l: (0, 0)),
            pl.BlockSpec((1, d, n), lambda i, l: (l, 0, 0)),
        ],
        out_specs=pl.BlockSpec((1, tr, n), lambda i, l: (l, i, 0)),
        out_shape=jax.ShapeDtypeStruct((depth, rows, n), BF16),
        compiler_params=_cparams("parallel", "arbitrary"),
        name="memkv",
    )(mem2d, g, w_kv)


def _xattn_kernel(x_ref, g_ref, wq_ref, k_ref, v_ref, wo_ref, o_ref):
    x = x_ref[0]
    hn = _rms(x, g_ref[...]).astype(BF16)
    q = _dot(hn, wq_ref[...])
    scale = X_HEAD_DIM ** -0.5
    outs = []
    for h in range(X_HEADS):
        sl = slice(h * X_HEAD_DIM, (h + 1) * X_HEAD_DIM)
        logits = _dot_nt(q[:, sl].astype(BF16), k_ref[0, :, sl]) * scale
        m = jnp.max(logits, axis=-1, keepdims=True)
        e = jnp.exp(logits - m)
        s = jnp.sum(e, axis=-1, keepdims=True)
        oh = _dot(e.astype(BF16), v_ref[0, :, sl]) / s
        outs.append(oh.astype(BF16))
    a = jnp.concatenate(outs, axis=-1)
    o_ref[0] = x + _dot(a, wo_ref[...])


def _xattn(x, g, wq, kv, wo):
    b, s, d = x.shape
    m = kv.shape[1]
    inner = X_HEADS * X_HEAD_DIM
    ts = min(512, s)
    return pl.pallas_call(
        _xattn_kernel,
        grid=(b, s // ts),
        in_specs=[
            pl.BlockSpec((1, ts, d), lambda bi, i: (bi, i, 0)),
            _const_spec((1, d)),
            _const_spec((d, inner)),
            pl.BlockSpec((1, m, inner), lambda bi, i: (bi, 0, 0)),
            pl.BlockSpec((1, m, inner), lambda bi, i: (bi, 0, 1)),
            _const_spec((inner, d)),
        ],
        out_specs=pl.BlockSpec((1, ts, d), lambda bi, i: (bi, i, 0)),
        out_shape=jax.ShapeDtypeStruct(x.shape, F32),
        compiler_params=_cparams("parallel", "arbitrary"),
        name="xattn",
    )(x, g, wq, kv, kv, wo)


def _ffn_kernel(x_ref, g_ref, wg_ref, wu_ref, wo_ref, o_ref, hn_ref):
    j = pl.program_id(1)

    @pl.when(j == 0)
    def _():
        x = x_ref[...]
        hn_ref[...] = _rms(x, g_ref[...]).astype(BF16)
        o_ref[...] = x

    hn = hn_ref[...]
    gate = _dot(hn, wg_ref[...])
    up = _dot(hn, wu_ref[...])
    act = (gate / (1.0 + jnp.exp(-gate)) * up).astype(BF16)
    o_ref[...] += _dot(act, wo_ref[...])


def _ffn(x2d, g, w_in, w_out):
    rows, d = x2d.shape
    hidden = w_out.shape[0]
    tm = min(512, rows)
    th = 512 if hidden % 512 == 0 else 256
    nh = hidden // th
    return pl.pallas_call(
        _ffn_kernel,
        grid=(rows // tm, nh),
        in_specs=[
            pl.BlockSpec((tm, d), lambda i, j: (i, 0)),
            _const_spec((1, d)),
            pl.BlockSpec((d, th), lambda i, j: (0, j)),
            pl.BlockSpec((d, th), lambda i, j: (0, nh + j)),
            pl.BlockSpec((th, d), lambda i, j: (j, 0)),
        ],
        out_specs=pl.BlockSpec((tm, d), lambda i, j: (i, 0)),
        out_shape=jax.ShapeDtypeStruct(x2d.shape, F32),
        scratch_shapes=[pltpu.VMEM((tm, d), BF16)],
        compiler_params=_cparams("parallel", "arbitrary"),
        name="ffn",
    )(x2d, g, w_in, w_in, w_out)


def _pool_kernel(x_ref, halo_ref, g_ref, w_ref, sc_ref, o_ref, hp_ref, *, ts):
    i = pl.program_id(1)
    x = x_ref[0]
    g = g_ref[...]
    hn = _rms(x, g)
    hh = _rms(halo_ref[0], g) * jnp.where(i > 0, 1.0, 0.0)
    hp_ref[0:POOL_HALO, :] = hh
    hp_ref[POOL_HALO:, :] = hn
    pos = i * ts + lax.broadcasted_iota(jnp.int32, (ts, 1), 0)
    gc = x.shape[-1] // len(POOL_WINDOWS)
    for gi, win in enumerate(POOL_WINDOWS):
        cols = slice(gi * gc, (gi + 1) * gc)
        acc = hp_ref[:, cols]
        step = 1
        while step < win:
            acc = acc + pltpu.roll(acc, step, axis=0)
            step *= 2
        cnt = jnp.minimum(pos + 1, win).astype(F32)
        mean = acc[POOL_HALO:, :] / cnt
        diff = (mean - hn[:, cols]).astype(BF16)
        y = _dot(diff, w_ref[gi])
        o_ref[0, :, cols] = x[:, cols] + y * sc_ref[:, cols]


def _pool(x, g, w, sc):
    b, s, d = x.shape
    ts = min(512, s)
    hb = ts // POOL_HALO
    groups, gc, _ = w.shape
    return pl.pallas_call(
        functools.partial(_pool_kernel, ts=ts),
        grid=(b, s // ts),
        in_specs=[
            pl.BlockSpec((1, ts, d), lambda bi, i: (bi, i, 0)),
            pl.BlockSpec((1, POOL_HALO, d), lambda bi, i: (bi, jnp.maximum(i * hb - 1, 0), 0)),
            _const_spec((1, d)),
            _const_spec((groups, gc, gc)),
            _const_spec((1, d)),
        ],
        out_specs=pl.BlockSpec((1, ts, d), lambda bi, i: (bi, i, 0)),
        out_shape=jax.ShapeDtypeStruct(x.shape, F32),
        scratch_shapes=[pltpu.VMEM((ts + POOL_HALO, d), F32)],
        compiler_params=_cparams("parallel", "arbitrary"),
        name="pool",
    )(x, x, g, w, sc)


def _dsa_proj_kernel(x_ref, g_ref, wq_ref, wkT_ref, wv_ref, wqiT_ref, wki_ref, wkis_ref, wwiT_ref,
                     cq_ref, sq_ref, caT_ref, saT_ref, ciT_ref, siT_ref, cci_ref, ssi_ref,
                     q_ref, kT_ref, v_ref, qiT_ref, ki_ref, wiT_ref, *, n_heads):
    hn = _rms(x_ref[0], g_ref[...]).astype(BF16)

    q = _dot(hn, wq_ref[...])
    cq = cq_ref[0]
    sq = sq_ref[0]
    for h in range(n_heads):
        qh = q[:, h * HEAD_DIM:(h + 1) * HEAD_DIM]
        q_ref[0, h] = (qh * cq + pltpu.roll(qh, HEAD_DIM // 2, axis=1) * sq).astype(BF16)

    kT = _dot_nt(wkT_ref[...], hn)
    half = HEAD_DIM // 2
    k1, k2 = kT[:half], kT[half:]
    ca, sa = caT_ref[0], saT_ref[0]
    kT_ref[0, :half, :] = (k1 * ca - k2 * sa).astype(BF16)
    kT_ref[0, half:, :] = (k2 * ca + k1 * sa).astype(BF16)

    v_ref[0] = _dot(hn, wv_ref[...]).astype(BF16)

    qiT = _dot_nt(wqiT_ref[...], hn)
    ci, si = ciT_ref[0], siT_ref[0]
    ih = IDX_DIM // 2
    for h in range(IDX_HEADS):
        r0 = h * IDX_DIM
        x1, x2 = qiT[r0:r0 + ih], qiT[r0 + ih:r0 + IDX_DIM]
        qiT_ref[0, r0:r0 + ih, :] = (x1 * ci - x2 * si).astype(BF16)
        qiT_ref[0, r0 + ih:r0 + IDX_DIM, :] = (x2 * ci + x1 * si).astype(BF16)

    ki_ref[0] = (_dot(hn, wki_ref[...]) * cci_ref[0] + _dot(hn, wkis_ref[...]) * ssi_ref[0]).astype(BF16)

    wiT_ref[0] = _dot_nt(wwiT_ref[...], hn) * (IDX_HEADS ** -0.5 * IDX_DIM ** -0.5)


def _dsa_proj(x, g, w, tabs):
    b, s, d = x.shape
    n_heads = d // HEAD_DIM
    ts = min(512, s)
    qi_rows = IDX_HEADS * IDX_DIM

    def tile(shape, seq_axis):
        def imap(bi, i):
            idx = [0] * len(shape)
            idx[0] = bi
            idx[seq_axis] = i
            return tuple(idx)
        return pl.BlockSpec(shape, imap)

    in_specs = [
        tile((1, ts, d), 1),
        _const_spec((1, d)),
        _const_spec(w["wq"].shape), _const_spec(w["wkT"].shape), _const_spec(w["wv"].shape),
        _const_spec(w["wqiT"].shape), _const_spec(w["wki"].shape), _const_spec(w["wkis"].shape),
        _const_spec(w["wwiT"].shape),
        tile((1, ts, HEAD_DIM), 1), tile((1, ts, HEAD_DIM), 1),
        tile((1, HEAD_DIM // 2, ts), 2), tile((1, HEAD_DIM // 2, ts), 2),
        tile((1, IDX_DIM // 2, ts), 2), tile((1, IDX_DIM // 2, ts), 2),
        tile((1, ts, IDX_DIM), 1), tile((1, ts, IDX_DIM), 1),
    ]
    out_specs = [
        tile((1, n_heads, ts, HEAD_DIM), 2),
        tile((1, HEAD_DIM, ts), 2),
        tile((1, ts, HEAD_DIM), 1),
        tile((1, qi_rows, ts), 2),
        tile((1, ts, IDX_DIM), 1),
        tile((1, IDX_HEADS, ts), 2),
    ]
    out_shape = [
        jax.ShapeDtypeStruct((b, n_heads, s, HEAD_DIM), BF16),
        jax.ShapeDtypeStruct((b, HEAD_DIM, s), BF16),
        jax.ShapeDtypeStruct((b, s, HEAD_DIM), BF16),
        jax.ShapeDtypeStruct((b, qi_rows, s), BF16),
        jax.ShapeDtypeStruct((b, s, IDX_DIM), BF16),
        jax.ShapeDtypeStruct((b, IDX_HEADS, s), F32),
    ]
    return pl.pallas_call(
        functools.partial(_dsa_proj_kernel, n_heads=n_heads),
        grid=(b, s // ts),
        in_specs=in_specs,
        out_specs=out_specs,
        out_shape=out_shape,
        compiler_params=_cparams("parallel", "arbitrary"),
        name="dsa_proj",
    )(x, g, w["wq"], w["wkT"], w["wv"], w["wqiT"], w["wki"], w["wkis"], w["wwiT"],
      tabs["cq"], tabs["sq"], tabs["caT"], tabs["saT"], tabs["ciT"], tabs["siT"], tabs["cci"], tabs["ssi"])


SCORE_ROWS = 128
TIE_ROWS = 256


def _dsa_attn_kernel(qiT_ref, ki_ref, wiT_ref, q_ref, kT_ref, v_ref, *rest,
                     s_c, qn, topk, qb0, n_heads, head_group):
    o_ref, keys_ref, biasT_ref, bias_ref = rest[-4:]
    q0 = (pl.program_id(1) + qb0) * qn
    t_pos = q0 + lax.broadcasted_iota(jnp.int32, (1, qn), 1)
    i_zero, i_one, i_min = jnp.int32(0), jnp.int32(1), jnp.int32(INT_MIN)
    f_zero, f_one, f_mask = jnp.float32(0.0), jnp.float32(1.0), jnp.float32(MASK_BIAS)

    def score_chunk(c, carry):
        r0 = pl.multiple_of(c * SCORE_ROWS, SCORE_ROWS)
        kic = ki_ref[0, pl.ds(r0, SCORE_ROWS), :]
        acc = jnp.zeros((SCORE_ROWS, qn), F32)
        for h in range(IDX_HEADS):
            rel = _dot(kic, qiT_ref[0, h * IDX_DIM:(h + 1) * IDX_DIM, :])
            acc = acc + wiT_ref[0, h:h + 1, :] * jnp.maximum(rel, 0.0)
        bits = pltpu.bitcast(acc + 0.0, jnp.int32)
        key = bits ^ ((bits >> 31) & jnp.int32(0x7FFFFFFF))
        s_pos = r0 + lax.broadcasted_iota(jnp.int32, (SCORE_ROWS, 1), 0)
        keys_ref[pl.ds(r0, SCORE_ROWS), :] = jnp.where(s_pos <= t_pos, key, i_min)
        return carry

    lax.fori_loop(0, s_c // SCORE_ROWS, score_chunk, 0)

    kk = jnp.minimum(t_pos + 1, topk)

    def count(mask):
        return jnp.sum(jnp.where(mask, i_one, i_zero), axis=0, keepdims=True)

    ans = jnp.where(count(keys_ref[...] >= i_zero) >= kk, i_zero, i_min)

    def refine(i, ans):
        cand = ans | jnp.left_shift(i_one, 30 - i)
        return jnp.where(count(keys_ref[...] >= cand) >= kk, cand, ans)

    ans = lax.fori_loop(0, 31, refine, ans)

    keys = keys_ref[...]
    need = kk - count(keys > ans)
    biasT_ref[...] = jnp.where(keys >= ans, f_zero, f_mask)
    tie = jnp.max(jnp.where(count(keys == ans) > need, i_one, i_zero))

    @pl.when(tie > 0)
    def _():
        needf = need.astype(F32)
        rr = lax.broadcasted_iota(jnp.int32, (TIE_ROWS, TIE_ROWS), 0)
        cc = lax.broadcasted_iota(jnp.int32, (TIE_ROWS, TIE_ROWS), 1)
        tri = jnp.where(cc < rr, f_one, f_zero).astype(BF16)
        carry = jnp.zeros((1, qn), F32)
        for c in range(s_c // TIE_ROWS):
            rows = slice(c * TIE_ROWS, (c + 1) * TIE_ROWS)
            kc = keys_ref[rows, :]
            eq = jnp.where(kc == ans, f_one, f_zero)
            before = _dot(tri, eq.astype(BF16)) + carry
            keep = jnp.where(kc > ans, f_one, eq * jnp.where(before < needf, f_one, f_zero))
            biasT_ref[rows, :] = jnp.where(keep > 0.5, f_zero, f_mask)
            carry = carry + jnp.sum(eq, axis=0, keepdims=True)

    for c in range(s_c // qn):
        cols = slice(c * qn, (c + 1) * qn)
        bias_ref[:, cols] = biasT_ref[cols, :].T

    scale = HEAD_DIM ** -0.5
    bias = bias_ref[...]
    kT = kT_ref[0]
    v = v_ref[0]
    for hg in range(n_heads // head_group):
        qh = q_ref[0, hg * head_group:(hg + 1) * head_group].reshape(head_group * qn, HEAD_DIM)
        logits = _dot(qh, kT).reshape(head_group, qn, s_c) * scale + bias[None]
        m = jnp.max(logits, axis=-1, keepdims=True)
        e = jnp.exp(logits - m)
        ssum = jnp.sum(e, axis=-1, keepdims=True)
        o = _dot(e.astype(BF16).reshape(head_group * qn, s_c), v).reshape(head_group, qn, HEAD_DIM) / ssum
        for j in range(head_group):
            h = hg * head_group + j
            o_ref[0, :, h * HEAD_DIM:(h + 1) * HEAD_DIM] = o[j].astype(BF16)


def _dsa_attn(p, prev, *, qb0, nqb, qn, topk):
    q, kT, v, qiT, ki, wiT = p
    b, n_heads, s, _ = q.shape
    s_c = (qb0 + nqb) * qn
    d = n_heads * HEAD_DIM
    qi_rows = IDX_HEADS * IDX_DIM
    kern = functools.partial(_dsa_attn_kernel, s_c=s_c, qn=qn, topk=topk, qb0=qb0,
                             n_heads=n_heads, head_group=2)
    in_specs = [
        pl.BlockSpec((1, qi_rows, qn), lambda bi, i: (bi, 0, qb0 + i)),
        pl.BlockSpec((1, s_c, IDX_DIM), lambda bi, i: (bi, 0, 0)),
        pl.BlockSpec((1, IDX_HEADS, qn), lambda bi, i: (bi, 0, qb0 + i)),
        pl.BlockSpec((1, n_heads, qn, HEAD_DIM), lambda bi, i: (bi, 0, qb0 + i, 0)),
        pl.BlockSpec((1, HEAD_DIM, s_c), lambda bi, i: (bi, 0, 0)),
        pl.BlockSpec((1, s_c, HEAD_DIM), lambda bi, i: (bi, 0, 0)),
    ]
    args = [qiT, ki, wiT, q, kT, v]
    aliases = {}
    if prev is not None:
        in_specs.append(pl.BlockSpec(memory_space=pl.ANY))
        args.append(prev)
        aliases = {len(args) - 1: 0}
    return pl.pallas_call(
        kern,
        grid=(b, nqb),
        in_specs=in_specs,
        out_specs=pl.BlockSpec((1, qn, d), lambda bi, i: (bi, qb0 + i, 0)),
        out_shape=jax.ShapeDtypeStruct((b, s, d), BF16),
        scratch_shapes=[
            pltpu.VMEM((s_c, qn), jnp.int32),
            pltpu.VMEM((s_c, qn), F32),
            pltpu.VMEM((qn, s_c), F32),
        ],
        input_output_aliases=aliases,
        compiler_params=_cparams("parallel", "arbitrary"),
        name=f"dsa_attn_{s_c}",
    )(*args)


def _proj_res_kernel(x_ref, a_ref, w_ref, o_ref):
    o_ref[...] = x_ref[...] + _dot(a_ref[...], w_ref[...])


def _proj_res(x2d, a2d, w):
    rows, d = x2d.shape
    k = a2d.shape[1]
    tm = min(512, rows)
    return pl.pallas_call(
        _proj_res_kernel,
        grid=(rows // tm,),
        in_specs=[
            pl.BlockSpec((tm, d), lambda i: (i, 0)),
            pl.BlockSpec((tm, k), lambda i: (i, 0)),
            _const_spec((k, d)),
        ],
        out_specs=pl.BlockSpec((tm, d), lambda i: (i, 0)),
        out_shape=jax.ShapeDtypeStruct(x2d.shape, F32),
        compiler_params=_cparams("parallel"),
        name="proj_res",
    )(x2d, a2d, w)


def _norm_kernel(x_ref, g_ref, o_ref):
    o_ref[...] = _rms(x_ref[...], g_ref[...])


def _final_norm(x2d, g):
    rows, d = x2d.shape
    tm = min(512, rows)
    return pl.pallas_call(
        _norm_kernel,
        grid=(rows // tm,),
        in_specs=[pl.BlockSpec((tm, d), lambda i: (i, 0)), _const_spec((1, d))],
        out_specs=pl.BlockSpec((tm, d), lambda i: (i, 0)),
        out_shape=jax.ShapeDtypeStruct(x2d.shape, F32),
        compiler_params=_cparams("parallel"),
        name="final_norm",
    )(x2d, g)


def _rope_tables(positions):
    def cs(dim):
        inv_freq = ROPE_THETA ** (-(jnp.arange(0, dim, 2, dtype=F32) / dim))
        ang = positions.astype(F32)[..., None] * inv_freq
        return jnp.cos(ang), jnp.sin(ang)

    cos_a, sin_a = cs(HEAD_DIM)
    cos_i, sin_i = cs(IDX_DIM)
    tr = lambda t: jnp.swapaxes(t, 1, 2)
    return {
        "cq": jnp.concatenate([cos_a, cos_a], -1), "sq": jnp.concatenate([-sin_a, sin_a], -1),
        "caT": tr(cos_a), "saT": tr(sin_a),
        "ciT": tr(cos_i), "siT": tr(sin_i),
        "cci": jnp.concatenate([cos_i, cos_i], -1), "ssi": jnp.concatenate([sin_i, sin_i], -1),
    }


def _dsa_weights(w_in, d):
    n_q = d
    o = 0
    wq = w_in[:, o:o + n_q]; o += n_q
    wk = w_in[:, o:o + HEAD_DIM]; o += HEAD_DIM
    wv = w_in[:, o:o + HEAD_DIM]; o += HEAD_DIM
    wqi = w_in[:, o:o + IDX_HEADS * IDX_DIM]; o += IDX_HEADS * IDX_DIM
    wki = w_in[:, o:o + IDX_DIM]; o += IDX_DIM
    wwi = w_in[:, o:o + IDX_HEADS]
    ih = IDX_DIM // 2
    wkis = jnp.concatenate([-wki[:, ih:], wki[:, :ih]], axis=1)
    c = lambda t: t.astype(BF16)
    return {"wq": c(wq), "wkT": c(wk.T), "wv": c(wv), "wqiT": c(wqi.T), "wki": c(wki), "wkis": c(wkis),
            "wwiT": c(wwi.T)}


def kernel(x, mem, positions, norm_mix, norm_xattn, norm_ffn, norm_memory, norm_final, pool_w, pool_scale,
           attn_w_in, attn_w_out, xattn_w_q, xattn_w_kv, xattn_w_o, ffn_w_in, ffn_w_out):
    b, s, d = x.shape
    depth = norm_mix.shape[0]
    m = mem.shape[1]
    row = lambda t: t.reshape(1, d)
    c = lambda t: t.astype(BF16)

    tabs = _rope_tables(positions)
    kv_all = _memkv(mem.reshape(b * m, d), row(norm_memory), c(xattn_w_kv)).reshape(depth, b, m, -1)

    topk = min(TOPK_MAX, s // 4)
    qn = min(256, s)
    group = 2 if (s // qn) % 2 == 0 else 1

    ia = ib = 0
    for i in range(depth):
        if i % 2 == 0:
            x = _pool(x, row(norm_mix[i]), c(pool_w[ia]), row(pool_scale[ia]))
            ia += 1
        else:
            p = _dsa_proj(x, row(norm_mix[i]), _dsa_weights(attn_w_in[ib], d), tabs)
            a = None
            for qb0 in range(0, s // qn, group):
                a = _dsa_attn(p, a, qb0=qb0, nqb=group, qn=qn, topk=topk)
            x = _proj_res(x.reshape(b * s, d), a.reshape(b * s, d), c(attn_w_out[ib])).reshape(b, s, d)
            ib += 1
        x = _xattn(x, row(norm_xattn[i]), c(xattn_w_q[i]), kv_all[i], c(xattn_w_o[i]))
        x = _ffn(x.reshape(b * s, d), row(norm_ffn[i]), c(ffn_w_in[i]), c(ffn_w_out[i])).reshape(b, s, d)
    return _final_norm(x.reshape(b * s, d), row(norm_final)).reshape(b, s, d)
```

```python
import functools
import math

import jax
import jax.numpy as jnp
from jax import lax
from jax.experimental import pallas as pl
from jax.experimental.pallas import tpu as pltpu

EPS = 1e-6
ROPE_THETA = 10000.0
POOL_WINDOWS = (2, 4, 8, 16)
POOL_HALO = 16
HEAD_DIM = 128
IDX_HEADS = 16
IDX_DIM = 64
TOPK_MAX = 256
X_HEADS = 4
X_HEAD_DIM = 128

BF16 = jnp.bfloat16
F32 = jnp.float32
INT_MIN = -(2**31)
MASK_BIAS = -1e30
VMEM_LIMIT_BYTES = 56 * 1024 * 1024
FFN_ROWS = 1024

_NT = (((1,), (1,)), ((), ()))


def _cparams(*sem):
    return pltpu.CompilerParams(dimension_semantics=sem, vmem_limit_bytes=VMEM_LIMIT_BYTES)


def _rms(x, g):
    ms = jnp.mean(x * x, axis=-1, keepdims=True)
    return x * lax.rsqrt(ms + EPS) * g


def _dot(a, b):
    return jnp.dot(a, b, preferred_element_type=F32)


def _dot_nt(a, b):
    return lax.dot_general(a, b, _NT, preferred_element_type=F32)


def _const_spec(shape):
    n = len(shape)
    return pl.BlockSpec(shape, lambda *_: (0,) * n, pipeline_mode=pl.Buffered(1))


def _memkv_kernel(mem_ref, g_ref, w_ref, o_ref):
    hn = _rms(mem_ref[...], g_ref[...]).astype(BF16)
    o_ref[0] = _dot(hn, w_ref[0]).astype(BF16)


def _memkv(mem2d, g, w_kv):
    rows, d = mem2d.shape
    depth, _, n = w_kv.shape
    tr = min(512, rows)
    return pl.pallas_call(
        _memkv_kernel,
        grid=(rows // tr, depth),
        in_specs=[
            pl.BlockSpec((tr, d), lambda i, l: (i, 0)),
            pl.BlockSpec((1, d), lambda i, l: (0, 0)),
            pl.BlockSpec((1, d, n), lambda i, l: (l, 0, 0)),
        ],
        out_specs=pl.BlockSpec((1, tr, n), lambda i, l: (l, i, 0)),
        out_shape=jax.ShapeDtypeStruct((depth, rows, n), BF16),
        compiler_params=_cparams("parallel", "arbitrary"),
        name="memkv",
    )(mem2d, g, w_kv)


def _xattn_kernel(x_ref, g_ref, wq_ref, k_ref, v_ref, wo_ref, o_ref):
    x = x_ref[0]
    hn = _rms(x, g_ref[...]).astype(BF16)
    q = _dot(hn, wq_ref[...])
    exp2_scale = X_HEAD_DIM ** -0.5 * math.log2(math.e)
    outs = []
    for h in range(X_HEADS):
        sl = slice(h * X_HEAD_DIM, (h + 1) * X_HEAD_DIM)
        logits = _dot_nt(q[:, sl].astype(BF16), k_ref[0, :, sl])
        m = jnp.max(logits, axis=-1, keepdims=True)
        e = jnp.exp2((logits - m) * exp2_scale)
        s = jnp.sum(e, axis=-1, keepdims=True)
        oh = _dot(e.astype(BF16), v_ref[0, :, sl]) / s
        outs.append(oh.astype(BF16))
    a = jnp.concatenate(outs, axis=-1)
    o_ref[0] = x + _dot(a, wo_ref[...])


def _xattn(x, g, wq, kv, wo):
    b, s, d = x.shape
    m = kv.shape[1]
    inner = X_HEADS * X_HEAD_DIM
    ts = min(512, s)
    return pl.pallas_call(
        _xattn_kernel,
        grid=(b, s // ts),
        in_specs=[
            pl.BlockSpec((1, ts, d), lambda bi, i: (bi, i, 0)),
            _const_spec((1, d)),
            _const_spec((d, inner)),
            pl.BlockSpec((1, m, inner), lambda bi, i: (bi, 0, 0)),
            pl.BlockSpec((1, m, inner), lambda bi, i: (bi, 0, 1)),
            _const_spec((inner, d)),
        ],
        out_specs=pl.BlockSpec((1, ts, d), lambda bi, i: (bi, i, 0)),
        out_shape=jax.ShapeDtypeStruct(x.shape, F32),
        compiler_params=_cparams("parallel", "arbitrary"),
        name="xattn",
    )(x, g, wq, kv, kv, wo)


def _ffn_kernel(x_ref, g_ref, wg_ref, wu_ref, wo_ref, *rest):
    o_ref, hn_ref = rest[-2:]
    j = pl.program_id(1)

    @pl.when(j == 0)
    def _():
        x = x_ref[...]
        hn_ref[...] = _rms(x, g_ref[...]).astype(BF16)
        o_ref[...] = x

    hn = hn_ref[...]
    gate = _dot(hn, wg_ref[...])
    up = _dot(hn, wu_ref[...])
    act = (gate / (1.0 + jnp.exp(-gate)) * up).astype(BF16)
    o_ref[...] += _dot(act, wo_ref[...])

    if len(rest) == 3:
        @pl.when(j == pl.num_programs(1) - 1)
        def _():
            o_ref[...] = _rms(o_ref[...], rest[0][...])


def _ffn(x2d, g, w_in, w_out, g_out=None):
    rows, d = x2d.shape
    hidden = w_out.shape[0]
    tm = min(FFN_ROWS, rows)
    th = 512 if hidden % 512 == 0 else 256
    nh = hidden // th
    in_specs = [
        pl.BlockSpec((tm, d), lambda i, j: (i, 0)),
        _const_spec((1, d)),
        pl.BlockSpec((d, th), lambda i, j: (0, j)),
        pl.BlockSpec((d, th), lambda i, j: (0, nh + j)),
        pl.BlockSpec((th, d), lambda i, j: (j, 0)),
    ]
    args = [x2d, g, w_in, w_in, w_out]
    if g_out is not None:
        in_specs.append(_const_spec((1, d)))
        args.append(g_out)
    return pl.pallas_call(
        _ffn_kernel,
        grid=(rows // tm, nh),
        in_specs=in_specs,
        out_specs=pl.BlockSpec((tm, d), lambda i, j: (i, 0)),
        out_shape=jax.ShapeDtypeStruct(x2d.shape, F32),
        scratch_shapes=[pltpu.VMEM((tm, d), BF16)],
        compiler_params=_cparams("parallel", "arbitrary"),
        name="ffn",
    )(*args)


def _pool_kernel(x_ref, halo_ref, g_ref, w_ref, sc_ref, o_ref, hp_ref, *, ts):
    i = pl.program_id(1)
    x = x_ref[0]
    g = g_ref[...]
    hn = _rms(x, g)
    hh = _rms(halo_ref[0], g) * jnp.where(i > 0, 1.0, 0.0)
    hp_ref[0:POOL_HALO, :] = hh
    hp_ref[POOL_HALO:, :] = hn
    pos = i * ts + lax.broadcasted_iota(jnp.int32, (ts, 1), 0)
    gc = x.shape[-1] // len(POOL_WINDOWS)
    for gi, win in enumerate(POOL_WINDOWS):
        cols = slice(gi * gc, (gi + 1) * gc)
        acc = hp_ref[:, cols]
        step = 1
        while step < win:
            acc = acc + pltpu.roll(acc, step, axis=0)
            step *= 2
        cnt = jnp.minimum(pos + 1, win).astype(F32)
        mean = acc[POOL_HALO:, :] / cnt
        diff = (mean - hn[:, cols]).astype(BF16)
        y = _dot(diff, w_ref[gi])
        o_ref[0, :, cols] = x[:, cols] + y * sc_ref[:, cols]


def _pool(x, g, w, sc):
    b, s, d = x.shape
    ts = min(512, s)
    hb = ts // POOL_HALO
    groups, gc, _ = w.shape
    return pl.pallas_call(
        functools.partial(_pool_kernel, ts=ts),
        grid=(b, s // ts),
        in_specs=[
            pl.BlockSpec((1, ts, d), lambda bi, i: (bi, i, 0)),
            pl.BlockSpec((1, POOL_HALO, d), lambda bi, i: (bi, jnp.maximum(i * hb - 1, 0), 0)),
            _const_spec((1, d)),
            _const_spec((groups, gc, gc)),
            _const_spec((1, d)),
        ],
        out_specs=pl.BlockSpec((1, ts, d), lambda bi, i: (bi, i, 0)),
        out_shape=jax.ShapeDtypeStruct(x.shape, F32),
        scratch_shapes=[pltpu.VMEM((ts + POOL_HALO, d), F32)],
        compiler_params=_cparams("parallel", "arbitrary"),
        name="pool",
    )(x, x, g, w, sc)


def _dsa_proj_kernel(x_ref, g_ref, wq_ref, wkT_ref, wv_ref, wqiT_ref, wki_ref, wkis_ref, wwiT_ref,
                     cq_ref, sq_ref, caT_ref, saT_ref, ciT_ref, siT_ref, cci_ref, ssi_ref,
                     q_ref, kT_ref, v_ref, qiT_ref, ki_ref, wiT_ref, *, n_heads):
    hn = _rms(x_ref[0], g_ref[...]).astype(BF16)

    q = _dot(hn, wq_ref[...])
    cq = cq_ref[0]
    sq = sq_ref[0]
    for h in range(n_heads):
        qh = q[:, h * HEAD_DIM:(h + 1) * HEAD_DIM]
        q_ref[0, h] = (qh * cq + pltpu.roll(qh, HEAD_DIM // 2, axis=1) * sq).astype(BF16)

    kT = _dot_nt(wkT_ref[...], hn)
    half = HEAD_DIM // 2
    k1, k2 = kT[:half], kT[half:]
    ca, sa = caT_ref[0], saT_ref[0]
    kT_ref[0, :half, :] = (k1 * ca - k2 * sa).astype(BF16)
    kT_ref[0, half:, :] = (k2 * ca + k1 * sa).astype(BF16)

    v_ref[0] = _dot(hn, wv_ref[...]).astype(BF16)

    qiT = _dot_nt(wqiT_ref[...], hn)
    ci, si = ciT_ref[0], siT_ref[0]
    ih = IDX_DIM // 2
    for h in range(IDX_HEADS):
        r0 = h * IDX_DIM
        x1, x2 = qiT[r0:r0 + ih], qiT[r0 + ih:r0 + IDX_DIM]
        qiT_ref[0, r0:r0 + ih, :] = (x1 * ci - x2 * si).astype(BF16)
        qiT_ref[0, r0 + ih:r0 + IDX_DIM, :] = (x2 * ci + x1 * si).astype(BF16)

    ki_ref[0] = (_dot(hn, wki_ref[...]) * cci_ref[0] + _dot(hn, wkis_ref[...]) * ssi_ref[0]).astype(BF16)

    wiT_ref[0] = _dot_nt(wwiT_ref[...], hn) * (IDX_HEADS ** -0.5 * IDX_DIM ** -0.5)


def _dsa_proj(x, g, w, tabs):
    b, s, d = x.shape
    n_heads = d // HEAD_DIM
    ts = min(512, s)
    qi_rows = IDX_HEADS * IDX_DIM

    def tile(shape, seq_axis):
        def imap(bi, i):
            idx = [0] * len(shape)
            idx[0] = bi
            idx[seq_axis] = i
            return tuple(idx)
        return pl.BlockSpec(shape, imap)

    in_specs = [
        tile((1, ts, d), 1),
        _const_spec((1, d)),
        _const_spec(w["wq"].shape), _const_spec(w["wkT"].shape), _const_spec(w["wv"].shape),
        _const_spec(w["wqiT"].shape), _const_spec(w["wki"].shape), _const_spec(w["wkis"].shape),
        _const_spec(w["wwiT"].shape),
        tile((1, ts, HEAD_DIM), 1), tile((1, ts, HEAD_DIM), 1),
        tile((1, HEAD_DIM // 2, ts), 2), tile((1, HEAD_DIM // 2, ts), 2),
        tile((1, IDX_DIM // 2, ts), 2), tile((1, IDX_DIM // 2, ts), 2),
        tile((1, ts, IDX_DIM), 1), tile((1, ts, IDX_DIM), 1),
    ]
    out_specs = [
        tile((1, n_heads, ts, HEAD_DIM), 2),
        tile((1, HEAD_DIM, ts), 2),
        tile((1, ts, HEAD_DIM), 1),
        tile((1, qi_rows, ts), 2),
        tile((1, ts, IDX_DIM), 1),
        tile((1, IDX_HEADS, ts), 2),
    ]
    out_shape = [
        jax.ShapeDtypeStruct((b, n_heads, s, HEAD_DIM), BF16),
        jax.ShapeDtypeStruct((b, HEAD_DIM, s), BF16),
        jax.ShapeDtypeStruct((b, s, HEAD_DIM), BF16),
        jax.ShapeDtypeStruct((b, qi_rows, s), BF16),
        jax.ShapeDtypeStruct((b, s, IDX_DIM), BF16),
        jax.ShapeDtypeStruct((b, IDX_HEADS, s), F32),
    ]
    return pl.pallas_call(
        functools.partial(_dsa_proj_kernel, n_heads=n_heads),
        grid=(b, s // ts),
        in_specs=in_specs,
        out_specs=out_specs,
        out_shape=out_shape,
        compiler_params=_cparams("parallel", "arbitrary"),
        name="dsa_proj",
    )(x, g, w["wq"], w["wkT"], w["wv"], w["wqiT"], w["wki"], w["wkis"], w["wwiT"],
      tabs["cq"], tabs["sq"], tabs["caT"], tabs["saT"], tabs["ciT"], tabs["siT"], tabs["cci"], tabs["ssi"])


SCORE_ROWS = 128
TIE_ROWS = 256
PLANE_ROWS = 256


def _bit_transpose32(words):
    words = list(words)
    j, m = 16, 0x0000FFFF
    while j:
        mask = jnp.int32(m)
        for k in range(32):
            if not k & j:
                t = (words[k] ^ (words[k + j] >> j)) & mask
                words[k] = words[k] ^ t
                words[k + j] = words[k + j] ^ (t << j)
        j >>= 1
        m = (m ^ (m << j)) & 0xFFFFFFFF
    return words


def _dsa_attn_kernel(qiT_ref, ki_ref, wiT_ref, q_ref, kT_ref, v_ref, *rest,
                     s_c, qn, topk, qb0, n_heads, head_group):
    o_ref, keys_ref, planes_ref, biasT_ref, bias_ref = rest[-5:]
    q0 = (pl.program_id(1) + qb0) * qn
    t_pos = q0 + lax.broadcasted_iota(jnp.int32, (1, qn), 1)
    i_zero, i_one, i_min = jnp.int32(0), jnp.int32(1), jnp.int32(INT_MIN)
    f_zero, f_one, f_mask = jnp.float32(0.0), jnp.float32(1.0), jnp.float32(MASK_BIAS)

    def score_chunk(c, carry):
        r0 = pl.multiple_of(c * SCORE_ROWS, SCORE_ROWS)
        kic = ki_ref[0, pl.ds(r0, SCORE_ROWS), :]
        acc = jnp.zeros((SCORE_ROWS, qn), F32)
        for h in range(IDX_HEADS):
            rel = _dot(kic, qiT_ref[0, h * IDX_DIM:(h + 1) * IDX_DIM, :])
            acc = acc + wiT_ref[0, h:h + 1, :] * jnp.maximum(rel, 0.0)
        bits = pltpu.bitcast(acc + 0.0, jnp.int32)
        key = bits ^ ((bits >> 31) & jnp.int32(0x7FFFFFFF))
        s_pos = r0 + lax.broadcasted_iota(jnp.int32, (SCORE_ROWS, 1), 0)
        keys_ref[pl.ds(r0, SCORE_ROWS), :] = jnp.where(s_pos <= t_pos, key, i_min)
        return carry

    lax.fori_loop(0, s_c // SCORE_ROWS, score_chunk, 0)

    def plane_block(bi, carry):
        r0 = bi * PLANE_ROWS
        p0 = pl.multiple_of(bi * 8, 8)
        for lt in range(qn // 128):
            lanes = slice(lt * 128, (lt + 1) * 128)
            words = [keys_ref[pl.ds(pl.multiple_of(r0 + 8 * j, 8), 8), lanes] ^ i_min for j in range(32)]
            for b, plane in enumerate(_bit_transpose32(words)):
                planes_ref[b, pl.ds(p0, 8), lanes] = plane
        return carry

    lax.fori_loop(0, s_c // PLANE_ROWS, plane_block, 0)

    kk = jnp.minimum(t_pos + 1, topk)

    def popcount_rows(words):
        return jnp.sum(lax.population_count(words), axis=0, keepdims=True)

    def refine(b, carry):
        alive, need, ans = carry
        ones = alive & planes_ref[b]
        cnt = popcount_rows(ones)
        take = cnt >= need
        alive = jnp.where(take, ones, alive ^ ones)
        need = jnp.where(take, need, need - cnt)
        ans = jnp.where(take, ans | jnp.left_shift(i_one, 31 - b), ans)
        return alive, need, ans

    alive = jnp.full((s_c // 32, qn), -1, jnp.int32)
    alive, need, ans = lax.fori_loop(0, 32, refine, (alive, kk, jnp.zeros((1, qn), jnp.int32)))
    ans = ans ^ i_min

    biasT_ref[...] = jnp.where(keys_ref[...] >= ans, f_zero, f_mask)
    tie = jnp.max(jnp.where(popcount_rows(alive) > need, i_one, i_zero))

    @pl.when(tie > 0)
    def _():
        needf = need.astype(F32)
        rr = lax.broadcasted_iota(jnp.int32, (TIE_ROWS, TIE_ROWS), 0)
        cc = lax.broadcasted_iota(jnp.int32, (TIE_ROWS, TIE_ROWS), 1)
        tri = jnp.where(cc < rr, f_one, f_zero).astype(BF16)
        carry = jnp.zeros((1, qn), F32)
        for c in range(s_c // TIE_ROWS):
            rows = slice(c * TIE_ROWS, (c + 1) * TIE_ROWS)
            kc = keys_ref[rows, :]
            eq = jnp.where(kc == ans, f_one, f_zero)
            before = _dot(tri, eq.astype(BF16)) + carry
            keep = jnp.where(kc > ans, f_one, eq * jnp.where(before < needf, f_one, f_zero))
            biasT_ref[rows, :] = jnp.where(keep > 0.5, f_zero, f_mask)
            carry = carry + jnp.sum(eq, axis=0, keepdims=True)

    for c in range(s_c // qn):
        cols = slice(c * qn, (c + 1) * qn)
        bias_ref[:, cols] = biasT_ref[cols, :].T

    exp2_scale = HEAD_DIM ** -0.5 * math.log2(math.e)
    bias = bias_ref[...]
    kT = kT_ref[0]
    v = v_ref[0]
    for hg in range(n_heads // head_group):
        qh = q_ref[0, hg * head_group:(hg + 1) * head_group].reshape(head_group * qn, HEAD_DIM)
        logits = _dot(qh, kT).reshape(head_group, qn, s_c) + bias[None]
        m = jnp.max(logits, axis=-1, keepdims=True)
        e = jnp.exp2((logits - m) * exp2_scale)
        ssum = jnp.sum(e, axis=-1, keepdims=True)
        o = _dot(e.astype(BF16).reshape(head_group * qn, s_c), v).reshape(head_group, qn, HEAD_DIM) / ssum
        for j in range(head_group):
            h = hg * head_group + j
            o_ref[0, :, h * HEAD_DIM:(h + 1) * HEAD_DIM] = o[j].astype(BF16)


def _dsa_attn(p, prev, *, qb0, nqb, qn, topk):
    q, kT, v, qiT, ki, wiT = p
    b, n_heads, s, _ = q.shape
    s_c = (qb0 + nqb) * qn
    d = n_heads * HEAD_DIM
    qi_rows = IDX_HEADS * IDX_DIM
    kern = functools.partial(_dsa_attn_kernel, s_c=s_c, qn=qn, topk=topk, qb0=qb0,
                             n_heads=n_heads, head_group=2)
    in_specs = [
        pl.BlockSpec((1, qi_rows, qn), lambda bi, i: (bi, 0, qb0 + i)),
        pl.BlockSpec((1, s_c, IDX_DIM), lambda bi, i: (bi, 0, 0)),
        pl.BlockSpec((1, IDX_HEADS, qn), lambda bi, i: (bi, 0, qb0 + i)),
        pl.BlockSpec((1, n_heads, qn, HEAD_DIM), lambda bi, i: (bi, 0, qb0 + i, 0)),
        pl.BlockSpec((1, HEAD_DIM, s_c), lambda bi, i: (bi, 0, 0)),
        pl.BlockSpec((1, s_c, HEAD_DIM), lambda bi, i: (bi, 0, 0)),
    ]
    args = [qiT, ki, wiT, q, kT, v]
    aliases = {}
    if prev is not None:
        in_specs.append(pl.BlockSpec(memory_space=pl.ANY))
        args.append(prev)
        aliases = {len(args) - 1: 0}
    return pl.pallas_call(
        kern,
        grid=(b, nqb),
        in_specs=in_specs,
        out_specs=pl.BlockSpec((1, qn, d), lambda bi, i: (bi, qb0 + i, 0)),
        out_shape=jax.ShapeDtypeStruct((b, s, d), BF16),
        scratch_shapes=[
            pltpu.VMEM((s_c, qn), jnp.int32),
            pltpu.VMEM((32, s_c // 32, qn), jnp.int32),
            pltpu.VMEM((s_c, qn), F32),
            pltpu.VMEM((qn, s_c), F32),
        ],
        input_output_aliases=aliases,
        compiler_params=_cparams("parallel", "arbitrary"),
        name=f"dsa_attn_{s_c}",
    )(*args)


def _proj_res_kernel(x_ref, a_ref, w_ref, o_ref):
    o_ref[...] = x_ref[...] + _dot(a_ref[...], w_ref[...])


def _proj_res(x2d, a2d, w):
    rows, d = x2d.shape
    k = a2d.shape[1]
    tm = min(512, rows)
    return pl.pallas_call(
        _proj_res_kernel,
        grid=(rows // tm,),
        in_specs=[
            pl.BlockSpec((tm, d), lambda i: (i, 0)),
            pl.BlockSpec((tm, k), lambda i: (i, 0)),
            _const_spec((k, d)),
        ],
        out_specs=pl.BlockSpec((tm, d), lambda i: (i, 0)),
        out_shape=jax.ShapeDtypeStruct(x2d.shape, F32),
        compiler_params=_cparams("parallel"),
        name="proj_res",
    )(x2d, a2d, w)


def _rope_tables(positions):
    def cs(dim):
        inv_freq = ROPE_THETA ** (-(jnp.arange(0, dim, 2, dtype=F32) / dim))
        ang = positions.astype(F32)[..., None] * inv_freq
        return jnp.cos(ang), jnp.sin(ang)

    cos_a, sin_a = cs(HEAD_DIM)
    cos_i, sin_i = cs(IDX_DIM)
    tr = lambda t: jnp.swapaxes(t, 1, 2)
    return {
        "cq": jnp.concatenate([cos_a, cos_a], -1), "sq": jnp.concatenate([-sin_a, sin_a], -1),
        "caT": tr(cos_a), "saT": tr(sin_a),
        "ciT": tr(cos_i), "siT": tr(sin_i),
        "cci": jnp.concatenate([cos_i, cos_i], -1), "ssi": jnp.concatenate([sin_i, sin_i], -1),
    }


def _dsa_weights(w_in, d):
    n_q = d
    o = 0
    wq = w_in[:, o:o + n_q]; o += n_q
    wk = w_in[:, o:o + HEAD_DIM]; o += HEAD_DIM
    wv = w_in[:, o:o + HEAD_DIM]; o += HEAD_DIM
    wqi = w_in[:, o:o + IDX_HEADS * IDX_DIM]; o += IDX_HEADS * IDX_DIM
    wki = w_in[:, o:o + IDX_DIM]; o += IDX_DIM
    wwi = w_in[:, o:o + IDX_HEADS]
    ih = IDX_DIM // 2
    wkis = jnp.concatenate([-wki[:, ih:], wki[:, :ih]], axis=1)
    c = lambda t: t.astype(BF16)
    return {"wq": c(wq), "wkT": c(wk.T), "wv": c(wv), "wqiT": c(wqi.T), "wki": c(wki), "wkis": c(wkis),
            "wwiT": c(wwi.T)}


def kernel(x, mem, positions, norm_mix, norm_xattn, norm_ffn, norm_memory, norm_final, pool_w, pool_scale,
           attn_w_in, attn_w_out, xattn_w_q, xattn_w_kv, xattn_w_o, ffn_w_in, ffn_w_out):
    b, s, d = x.shape
    depth = norm_mix.shape[0]
    m = mem.shape[1]
    row = lambda t: t.reshape(1, d)
    c = lambda t: t.astype(BF16)

    tabs = _rope_tables(positions)
    kv_all = _memkv(mem.reshape(b * m, d), row(norm_memory), c(xattn_w_kv)).reshape(depth, b, m, -1)

    topk = min(TOPK_MAX, s // 4)
    qn = min(256, s)
    group = 1

    ia = ib = 0
    for i in range(depth):
        if i % 2 == 0:
            x = _pool(x, row(norm_mix[i]), c(pool_w[ia]), row(pool_scale[ia]))
            ia += 1
        else:
            p = _dsa_proj(x, row(norm_mix[i]), _dsa_weights(attn_w_in[ib], d), tabs)
            a = None
            for qb0 in range(0, s // qn, group):
                a = _dsa_attn(p, a, qb0=qb0, nqb=group, qn=qn, topk=topk)
            x = _proj_res(x.reshape(b * s, d), a.reshape(b * s, d), c(attn_w_out[ib])).reshape(b, s, d)
            ib += 1
        x = _xattn(x, row(norm_xattn[i]), c(xattn_w_q[i]), kv_all[i], c(xattn_w_o[i]))
        g_out = row(norm_final) if i == depth - 1 else None
        x = _ffn(x.reshape(b * s, d), row(norm_ffn[i]), c(ffn_w_in[i]), c(ffn_w_out[i]), g_out).reshape(b, s, d)
    return x
```

```python
import functools
import math

import jax
import jax.numpy as jnp
from jax import lax
from jax.experimental import pallas as pl
from jax.experimental.pallas import tpu as pltpu

EPS = 1e-6
ROPE_THETA = 10000.0
POOL_WINDOWS = (2, 4, 8, 16)
POOL_HALO = 16
HEAD_DIM = 128
IDX_HEADS = 16
IDX_DIM = 64
TOPK_MAX = 256
X_HEADS = 4
X_HEAD_DIM = 128

BF16 = jnp.bfloat16
F32 = jnp.float32
INT_MIN = -(2**31)
MASK_BIAS = -1e30
VMEM_LIMIT_BYTES = 56 * 1024 * 1024
SEQ_ROWS = 512
FFN_ROWS = 1024
FFN_COLS = 512

_NT = (((1,), (1,)), ((), ()))


def _cparams(*sem):
    return pltpu.CompilerParams(dimension_semantics=sem, vmem_limit_bytes=VMEM_LIMIT_BYTES)


def _rms(x, g):
    ms = jnp.mean(x * x, axis=-1, keepdims=True)
    return x * lax.rsqrt(ms + EPS) * g


def _dot(a, b):
    return jnp.dot(a, b, preferred_element_type=F32)


def _dot_nt(a, b):
    return lax.dot_general(a, b, _NT, preferred_element_type=F32)


def _const_spec(shape, layer=None):
    n = len(shape)
    if layer is None:
        return pl.BlockSpec(shape, lambda *_: (0,) * n, pipeline_mode=pl.Buffered(1))
    return pl.BlockSpec((None,) + tuple(shape), lambda *_: (layer,) + (0,) * n, pipeline_mode=pl.Buffered(1))


def _seq_tile(shape, seq_axis):
    def imap(bi, i):
        idx = [0] * len(shape)
        idx[0] = bi
        idx[seq_axis] = i
        return tuple(idx)
    return pl.BlockSpec(shape, imap)


def _memkv_kernel(mem_ref, g_ref, w_ref, o_ref):
    hn = _rms(mem_ref[...], g_ref[...]).astype(BF16)
    o_ref[...] = _dot(hn, w_ref[...]).astype(BF16)


def _memkv(mem2d, g, w_kv):
    rows, d = mem2d.shape
    depth, _, n = w_kv.shape
    tr = min(512, rows)
    return pl.pallas_call(
        _memkv_kernel,
        grid=(rows // tr, depth),
        in_specs=[
            pl.BlockSpec((tr, d), lambda i, l: (i, 0)),
            pl.BlockSpec((1, d), lambda i, l: (0, 0)),
            pl.BlockSpec((None, d, n), lambda i, l: (l, 0, 0)),
        ],
        out_specs=pl.BlockSpec((None, tr, n), lambda i, l: (l, i, 0)),
        out_shape=jax.ShapeDtypeStruct((depth, rows, n), BF16),
        compiler_params=_cparams("parallel", "arbitrary"),
        name="memkv",
    )(mem2d, g, w_kv)


def _xattn_apply(x, g_ref, wq_ref, k_ref, v_ref, wo_ref):
    hn = _rms(x, g_ref[...]).astype(BF16)
    q = _dot(hn, wq_ref[...])
    exp2_scale = X_HEAD_DIM ** -0.5 * math.log2(math.e)
    outs = []
    for h in range(X_HEADS):
        sl = slice(h * X_HEAD_DIM, (h + 1) * X_HEAD_DIM)
        logits = _dot_nt(q[:, sl].astype(BF16), k_ref[0, :, sl])
        m = jnp.max(logits, axis=-1, keepdims=True)
        e = jnp.exp2((logits - m) * exp2_scale)
        s = jnp.sum(e, axis=-1, keepdims=True)
        oh = _dot(e.astype(BF16), v_ref[0, :, sl]) / s
        outs.append(oh.astype(BF16))
    a = jnp.concatenate(outs, axis=-1)
    return x + _dot(a, wo_ref[...])


def _xattn_operands(xa, d):
    layer, g, wq, kv, wo = xa
    m = kv.shape[2]
    inner = X_HEADS * X_HEAD_DIM
    specs = [
        _const_spec((1, d), layer),
        _const_spec((d, inner), layer),
        pl.BlockSpec((None, 1, m, inner), lambda bi, i: (layer, bi, 0, 0)),
        pl.BlockSpec((None, 1, m, inner), lambda bi, i: (layer, bi, 0, 1)),
        _const_spec((inner, d), layer),
    ]
    return specs, [g, wq, kv, kv, wo]


def _ffn_kernel(x_ref, g_ref, wg_ref, wu_ref, wo_ref, *rest):
    o_ref, hn_ref = rest[-2:]
    j = pl.program_id(1)

    @pl.when(j == 0)
    def _():
        x = x_ref[...]
        hn_ref[...] = _rms(x, g_ref[...]).astype(BF16)
        o_ref[...] = x

    hn = hn_ref[...]
    gate = _dot(hn, wg_ref[...])
    up = _dot(hn, wu_ref[...])
    act = (gate / (1.0 + jnp.exp(-gate)) * up).astype(BF16)
    o_ref[...] += _dot(act, wo_ref[...])

    if len(rest) == 3:
        @pl.when(j == pl.num_programs(1) - 1)
        def _():
            o_ref[...] = _rms(o_ref[...], rest[0][...])


def _ffn(x2d, layer, g, w_in, w_out, g_out=None):
    rows, d = x2d.shape
    hidden = w_out.shape[1]
    tm = min(FFN_ROWS, rows)
    th = FFN_COLS if hidden % FFN_COLS == 0 else 256
    nh = hidden // th
    in_specs = [
        pl.BlockSpec((tm, d), lambda i, j: (i, 0)),
        _const_spec((1, d), layer),
        pl.BlockSpec((None, d, th), lambda i, j: (layer, 0, j)),
        pl.BlockSpec((None, d, th), lambda i, j: (layer, 0, nh + j)),
        pl.BlockSpec((None, th, d), lambda i, j: (layer, j, 0)),
    ]
    args = [x2d, g, w_in, w_in, w_out]
    if g_out is not None:
        in_specs.append(_const_spec((1, d)))
        args.append(g_out)
    return pl.pallas_call(
        _ffn_kernel,
        grid=(rows // tm, nh),
        in_specs=in_specs,
        out_specs=pl.BlockSpec((tm, d), lambda i, j: (i, 0)),
        out_shape=jax.ShapeDtypeStruct(x2d.shape, F32),
        scratch_shapes=[pltpu.VMEM((tm, d), BF16)],
        compiler_params=_cparams("parallel", "arbitrary"),
        name="ffn",
    )(*args)


def _pool_xattn_kernel(x_ref, halo_ref, g_ref, w_ref, sc_ref, gx_ref, wq_ref, k_ref, v_ref, wo_ref,
                       o_ref, hp_ref, x1_ref, *, ts):
    i = pl.program_id(1)
    x = x_ref[0]
    g = g_ref[...]
    hn = _rms(x, g)
    hh = _rms(halo_ref[0], g) * jnp.where(i > 0, 1.0, 0.0)
    hp_ref[0:POOL_HALO, :] = hh
    hp_ref[POOL_HALO:, :] = hn
    pos = i * ts + lax.broadcasted_iota(jnp.int32, (ts, 1), 0)
    gc = x.shape[-1] // len(POOL_WINDOWS)
    for gi, win in enumerate(POOL_WINDOWS):
        cols = slice(gi * gc, (gi + 1) * gc)
        acc = hp_ref[:, cols]
        step = 1
        while step < win:
            acc = acc + pltpu.roll(acc, step, axis=0)
            step *= 2
        cnt = jnp.minimum(pos + 1, win).astype(F32)
        mean = acc[POOL_HALO:, :] / cnt
        diff = (mean - hn[:, cols]).astype(BF16)
        y = _dot(diff, w_ref[gi])
        x1_ref[:, cols] = x[:, cols] + y * sc_ref[:, cols]
    o_ref[0] = _xattn_apply(x1_ref[...], gx_ref, wq_ref, k_ref, v_ref, wo_ref)


def _pool_xattn(x, layer, g, pool_layer, w, sc, xa):
    b, s, d = x.shape
    ts = min(SEQ_ROWS, s)
    hb = ts // POOL_HALO
    _, groups, gc, _ = w.shape
    xa_specs, xa_args = _xattn_operands(xa, d)
    return pl.pallas_call(
        functools.partial(_pool_xattn_kernel, ts=ts),
        grid=(b, s // ts),
        in_specs=[
            pl.BlockSpec((1, ts, d), lambda bi, i: (bi, i, 0)),
            pl.BlockSpec((1, POOL_HALO, d), lambda bi, i: (bi, jnp.maximum(i * hb - 1, 0), 0)),
            _const_spec((1, d), layer),
            _const_spec((groups, gc, gc), pool_layer),
            _const_spec((1, d), pool_layer),
        ] + xa_specs,
        out_specs=pl.BlockSpec((1, ts, d), lambda bi, i: (bi, i, 0)),
        out_shape=jax.ShapeDtypeStruct(x.shape, F32),
        scratch_shapes=[pltpu.VMEM((ts + POOL_HALO, d), F32), pltpu.VMEM((ts, d), F32)],
        compiler_params=_cparams("parallel", "arbitrary"),
        name="pool_xattn",
    )(x, x, g, w, sc, *xa_args)


def _dsa_proj_kernel(x_ref, g_ref, wq_ref, wkT_ref, wsm_ref, wqiT_ref, wwiT_ref,
                     cq_ref, sq_ref, caT_ref, saT_ref, ciT_ref, siT_ref, cci_ref, ssi_ref,
                     q_ref, kT_ref, v_ref, qiT_ref, ki_ref, wiT_ref, *, n_heads):
    hn = _rms(x_ref[0], g_ref[...]).astype(BF16)

    q = _dot(hn, wq_ref[...])
    cq = cq_ref[0]
    sq = sq_ref[0]
    for h in range(n_heads):
        qh = q[:, h * HEAD_DIM:(h + 1) * HEAD_DIM]
        q_ref[0, h] = (qh * cq + pltpu.roll(qh, HEAD_DIM // 2, axis=1) * sq).astype(BF16)

    kT = _dot_nt(wkT_ref[...], hn)
    half = HEAD_DIM // 2
    k1, k2 = kT[:half], kT[half:]
    ca, sa = caT_ref[0], saT_ref[0]
    kT_ref[0, :half, :] = (k1 * ca - k2 * sa).astype(BF16)
    kT_ref[0, half:, :] = (k2 * ca + k1 * sa).astype(BF16)

    sm = _dot(hn, wsm_ref[...])
    v_ref[0] = sm[:, :HEAD_DIM].astype(BF16)
    ki = sm[:, HEAD_DIM:HEAD_DIM + IDX_DIM] * cci_ref[0] + sm[:, HEAD_DIM + IDX_DIM:] * ssi_ref[0]
    ki_ref[0] = ki.astype(BF16)

    qiT = _dot_nt(wqiT_ref[...], hn)
    ci, si = ciT_ref[0], siT_ref[0]
    ih = IDX_DIM // 2
    for h in range(IDX_HEADS):
        r0 = h * IDX_DIM
        x1, x2 = qiT[r0:r0 + ih], qiT[r0 + ih:r0 + IDX_DIM]
        qiT_ref[0, r0:r0 + ih, :] = (x1 * ci - x2 * si).astype(BF16)
        qiT_ref[0, r0 + ih:r0 + IDX_DIM, :] = (x2 * ci + x1 * si).astype(BF16)

    wiT_ref[0] = _dot_nt(wwiT_ref[...], hn) * (IDX_HEADS ** -0.5 * IDX_DIM ** -0.5)


def _dsa_proj(x, layer, g, w, tabs):
    b, s, d = x.shape
    n_heads = d // HEAD_DIM
    ts = min(SEQ_ROWS, s)
    qi_rows = IDX_HEADS * IDX_DIM
    in_specs = [
        _seq_tile((1, ts, d), 1),
        _const_spec((1, d), layer),
        _const_spec(w["wq"].shape), _const_spec(w["wkT"].shape), _const_spec(w["wsm"].shape),
        _const_spec(w["wqiT"].shape), _const_spec(w["wwiT"].shape),
        _seq_tile((1, ts, HEAD_DIM), 1), _seq_tile((1, ts, HEAD_DIM), 1),
        _seq_tile((1, HEAD_DIM // 2, ts), 2), _seq_tile((1, HEAD_DIM // 2, ts), 2),
        _seq_tile((1, IDX_DIM // 2, ts), 2), _seq_tile((1, IDX_DIM // 2, ts), 2),
        _seq_tile((1, ts, IDX_DIM), 1), _seq_tile((1, ts, IDX_DIM), 1),
    ]
    out_specs = [
        _seq_tile((1, n_heads, ts, HEAD_DIM), 2),
        _seq_tile((1, HEAD_DIM, ts), 2),
        _seq_tile((1, ts, HEAD_DIM), 1),
        _seq_tile((1, qi_rows, ts), 2),
        _seq_tile((1, ts, IDX_DIM), 1),
        _seq_tile((1, IDX_HEADS, ts), 2),
    ]
    out_shape = [
        jax.ShapeDtypeStruct((b, n_heads, s, HEAD_DIM), BF16),
        jax.ShapeDtypeStruct((b, HEAD_DIM, s), BF16),
        jax.ShapeDtypeStruct((b, s, HEAD_DIM), BF16),
        jax.ShapeDtypeStruct((b, qi_rows, s), BF16),
        jax.ShapeDtypeStruct((b, s, IDX_DIM), BF16),
        jax.ShapeDtypeStruct((b, IDX_HEADS, s), F32),
    ]
    return pl.pallas_call(
        functools.partial(_dsa_proj_kernel, n_heads=n_heads),
        grid=(b, s // ts),
        in_specs=in_specs,
        out_specs=out_specs,
        out_shape=out_shape,
        compiler_params=_cparams("parallel", "arbitrary"),
        name="dsa_proj",
    )(x, g, w["wq"], w["wkT"], w["wsm"], w["wqiT"], w["wwiT"],
      tabs["cq"], tabs["sq"], tabs["caT"], tabs["saT"], tabs["ciT"], tabs["siT"], tabs["cci"], tabs["ssi"])


SCORE_ROWS = 128
SCORE_UNROLL = 2
TIE_ROWS = 256
PLANE_ROWS = 256
HEAD_GROUP = 2


def _bit_transpose32(words):
    words = list(words)
    j, m = 16, 0x0000FFFF
    while j:
        mask = jnp.int32(m)
        for k in range(32):
            if not k & j:
                t = (words[k] ^ (words[k + j] >> j)) & mask
                words[k] = words[k] ^ t
                words[k + j] = words[k + j] ^ (t << j)
        j >>= 1
        m = (m ^ (m << j)) & 0xFFFFFFFF
    return words


def _dsa_attn_kernel(qiT_ref, ki_ref, wiT_ref, q_ref, kT_ref, v_ref, prev_ref, o_ref,
                     keys_ref, planes_ref, biasT_ref, bias_ref, *, s_c, qn, topk, qb0, n_heads):
    del prev_ref
    q0 = (pl.program_id(1) + qb0) * qn
    t_pos = q0 + lax.broadcasted_iota(jnp.int32, (1, qn), 1)
    i_zero, i_one, i_min = jnp.int32(0), jnp.int32(1), jnp.int32(INT_MIN)
    f_zero, f_one, f_mask = jnp.float32(0.0), jnp.float32(1.0), jnp.float32(MASK_BIAS)

    def score_chunk(c, carry):
        r0 = pl.multiple_of(c * SCORE_ROWS, SCORE_ROWS)
        kic = ki_ref[0, pl.ds(r0, SCORE_ROWS), :]
        acc = jnp.zeros((SCORE_ROWS, qn), F32)
        for h in range(IDX_HEADS):
            rel = _dot(kic, qiT_ref[0, h * IDX_DIM:(h + 1) * IDX_DIM, :])
            acc = acc + wiT_ref[0, h:h + 1, :] * jnp.maximum(rel, 0.0)
        bits = pltpu.bitcast(acc + 0.0, jnp.int32)
        key = bits ^ ((bits >> 31) & jnp.int32(0x7FFFFFFF))
        s_pos = r0 + lax.broadcasted_iota(jnp.int32, (SCORE_ROWS, 1), 0)
        keys_ref[pl.ds(r0, SCORE_ROWS), :] = jnp.where(s_pos <= t_pos, key, i_min)
        return carry

    lax.fori_loop(0, s_c // SCORE_ROWS, score_chunk, 0, unroll=SCORE_UNROLL)

    def plane_block(bi, carry):
        r0 = bi * PLANE_ROWS
        p0 = pl.multiple_of(bi * 8, 8)
        for lt in range(qn // 128):
            lanes = slice(lt * 128, (lt + 1) * 128)
            words = [keys_ref[pl.ds(pl.multiple_of(r0 + 8 * j, 8), 8), lanes] ^ i_min for j in range(32)]
            for b, plane in enumerate(_bit_transpose32(words)):
                planes_ref[b, pl.ds(p0, 8), lanes] = plane
        return carry

    lax.fori_loop(0, s_c // PLANE_ROWS, plane_block, 0)

    kk = jnp.minimum(t_pos + 1, topk)

    def popcount_rows(words):
        return jnp.sum(lax.population_count(words), axis=0, keepdims=True)

    def refine(b, carry):
        alive, need, ans = carry
        ones = alive & planes_ref[b]
        cnt = popcount_rows(ones)
        take = cnt >= need
        alive = jnp.where(take, ones, alive ^ ones)
        need = jnp.where(take, need, need - cnt)
        ans = jnp.where(take, ans | jnp.left_shift(i_one, 31 - b), ans)
        return alive, need, ans

    alive = jnp.full((s_c // 32, qn), -1, jnp.int32)
    alive, need, ans = lax.fori_loop(0, 32, refine, (alive, kk, jnp.zeros((1, qn), jnp.int32)))
    ans = ans ^ i_min

    biasT_ref[...] = jnp.where(keys_ref[...] >= ans, f_zero, f_mask)
    tie = jnp.max(jnp.where(popcount_rows(alive) > need, i_one, i_zero))

    @pl.when(tie > 0)
    def _():
        needf = need.astype(F32)
        rr = lax.broadcasted_iota(jnp.int32, (TIE_ROWS, TIE_ROWS), 0)
        cc = lax.broadcasted_iota(jnp.int32, (TIE_ROWS, TIE_ROWS), 1)
        tri = jnp.where(cc < rr, f_one, f_zero).astype(BF16)
        carry = jnp.zeros((1, qn), F32)
        for c in range(s_c // TIE_ROWS):
            rows = slice(c * TIE_ROWS, (c + 1) * TIE_ROWS)
            kc = keys_ref[rows, :]
            eq = jnp.where(kc == ans, f_one, f_zero)
            before = _dot(tri, eq.astype(BF16)) + carry
            keep = jnp.where(kc > ans, f_one, eq * jnp.where(before < needf, f_one, f_zero))
            biasT_ref[rows, :] = jnp.where(keep > 0.5, f_zero, f_mask)
            carry = carry + jnp.sum(eq, axis=0, keepdims=True)

    for c in range(s_c // qn):
        cols = slice(c * qn, (c + 1) * qn)
        bias_ref[:, cols] = biasT_ref[cols, :].T

    exp2_scale = HEAD_DIM ** -0.5 * math.log2(math.e)
    bias = bias_ref[...]
    kT = kT_ref[0]
    v = v_ref[0]
    hg = HEAD_GROUP
    for g0 in range(0, n_heads, hg):
        qh = q_ref[0, g0:g0 + hg].reshape(hg * qn, HEAD_DIM)
        logits = _dot(qh, kT).reshape(hg, qn, s_c) + bias[None]
        m = jnp.max(logits, axis=-1, keepdims=True)
        e = jnp.exp2((logits - m) * exp2_scale)
        ssum = jnp.sum(e, axis=-1, keepdims=True)
        o = _dot(e.astype(BF16).reshape(hg * qn, s_c), v).reshape(hg, qn, HEAD_DIM) / ssum
        for j in range(hg):
            o_ref[0, :, (g0 + j) * HEAD_DIM:(g0 + j + 1) * HEAD_DIM] = o[j].astype(BF16)


def _dsa_attn(p, prev, *, qb0, nqb, qn, topk):
    q, kT, v, qiT, ki, wiT = p
    b, n_heads, s, _ = q.shape
    s_c = (qb0 + nqb) * qn
    d = n_heads * HEAD_DIM
    qi_rows = IDX_HEADS * IDX_DIM
    kern = functools.partial(_dsa_attn_kernel, s_c=s_c, qn=qn, topk=topk, qb0=qb0, n_heads=n_heads)
    in_specs = [
        pl.BlockSpec((1, qi_rows, qn), lambda bi, i: (bi, 0, qb0 + i)),
        pl.BlockSpec((1, s_c, IDX_DIM), lambda bi, i: (bi, 0, 0)),
        pl.BlockSpec((1, IDX_HEADS, qn), lambda bi, i: (bi, 0, qb0 + i)),
        pl.BlockSpec((1, n_heads, qn, HEAD_DIM), lambda bi, i: (bi, 0, qb0 + i, 0)),
        pl.BlockSpec((1, HEAD_DIM, s_c), lambda bi, i: (bi, 0, 0)),
        pl.BlockSpec((1, s_c, HEAD_DIM), lambda bi, i: (bi, 0, 0)),
        pl.BlockSpec(memory_space=pl.ANY),
    ]
    return pl.pallas_call(
        kern,
        grid=(b, nqb),
        in_specs=in_specs,
        out_specs=pl.BlockSpec((1, qn, d), lambda bi, i: (bi, qb0 + i, 0)),
        out_shape=jax.ShapeDtypeStruct((b, s, d), BF16),
        scratch_shapes=[
            pltpu.VMEM((s_c, qn), jnp.int32),
            pltpu.VMEM((32, s_c // 32, qn), jnp.int32),
            pltpu.VMEM((s_c, qn), F32),
            pltpu.VMEM((qn, s_c), F32),
        ],
        input_output_aliases={6: 0},
        compiler_params=_cparams("parallel", "arbitrary"),
        name=f"dsa_attn_{s_c}",
    )(qiT, ki, wiT, q, kT, v, prev)


def _proj_xattn_kernel(x_ref, a_ref, w_ref, gx_ref, wq_ref, k_ref, v_ref, wo_ref, o_ref):
    x1 = x_ref[0] + _dot(a_ref[0], w_ref[...])
    o_ref[0] = _xattn_apply(x1, gx_ref, wq_ref, k_ref, v_ref, wo_ref)


def _proj_xattn(x, a, attn_layer, w, xa):
    b, s, d = x.shape
    k = a.shape[-1]
    ts = min(SEQ_ROWS, s)
    xa_specs, xa_args = _xattn_operands(xa, d)
    return pl.pallas_call(
        _proj_xattn_kernel,
        grid=(b, s // ts),
        in_specs=[
            pl.BlockSpec((1, ts, d), lambda bi, i: (bi, i, 0)),
            pl.BlockSpec((1, ts, k), lambda bi, i: (bi, i, 0)),
            _const_spec((k, d), attn_layer),
        ] + xa_specs,
        out_specs=pl.BlockSpec((1, ts, d), lambda bi, i: (bi, i, 0)),
        out_shape=jax.ShapeDtypeStruct(x.shape, F32),
        compiler_params=_cparams("parallel", "arbitrary"),
        name="proj_xattn",
    )(x, a, w, *xa_args)


def _rope_tables(positions):
    def cs(dim):
        inv_freq = ROPE_THETA ** (-(jnp.arange(0, dim, 2, dtype=F32) / dim))
        ang = positions.astype(F32)[..., None] * inv_freq
        return jnp.cos(ang), jnp.sin(ang)

    cos_a, sin_a, cos_i, sin_i = lax.optimization_barrier(cs(HEAD_DIM) + cs(IDX_DIM))
    tr = lambda t: jnp.swapaxes(t, 1, 2)
    return {
        "cq": jnp.concatenate([cos_a, cos_a], -1), "sq": jnp.concatenate([-sin_a, sin_a], -1),
        "caT": tr(cos_a), "saT": tr(sin_a),
        "ciT": tr(cos_i), "siT": tr(sin_i),
        "cci": jnp.concatenate([cos_i, cos_i], -1), "ssi": jnp.concatenate([sin_i, sin_i], -1),
    }


def _dsa_weights(w_in, d):
    n_q = d
    o = 0
    wq = w_in[:, o:o + n_q]; o += n_q
    wk = w_in[:, o:o + HEAD_DIM]; o += HEAD_DIM
    wv = w_in[:, o:o + HEAD_DIM]; o += HEAD_DIM
    wqi = w_in[:, o:o + IDX_HEADS * IDX_DIM]; o += IDX_HEADS * IDX_DIM
    wki = w_in[:, o:o + IDX_DIM]; o += IDX_DIM
    wwi = w_in[:, o:o + IDX_HEADS]
    ih = IDX_DIM // 2
    wsm = jnp.concatenate([wv, wki, -wki[:, ih:], wki[:, :ih]], axis=1)
    c = lambda t: t.astype(BF16)
    return {"wq": c(wq), "wkT": c(wk.T), "wsm": c(wsm), "wqiT": c(wqi.T), "wwiT": c(wwi.T)}


def kernel(x, mem, positions, norm_mix, norm_xattn, norm_ffn, norm_memory, norm_final, pool_w, pool_scale,
           attn_w_in, attn_w_out, xattn_w_q, xattn_w_kv, xattn_w_o, ffn_w_in, ffn_w_out):
    b, s, d = x.shape
    depth = norm_mix.shape[0]
    m = mem.shape[1]
    rows = lambda t: t.reshape(-1, 1, d)
    c = lambda t: t.astype(BF16)

    tabs = _rope_tables(positions)
    kv_all = _memkv(mem.reshape(b * m, d), norm_memory.reshape(1, d), c(xattn_w_kv)).reshape(depth, b, m, -1)
    g_mix, g_xattn, g_ffn = rows(norm_mix), rows(norm_xattn), rows(norm_ffn)
    w_pool, sc_pool = c(pool_w), rows(pool_scale)
    w_attn_out = c(attn_w_out)
    w_xq, w_xo = c(xattn_w_q), c(xattn_w_o)
    w_ffn_in, w_ffn_out = c(ffn_w_in), c(ffn_w_out)

    topk = min(TOPK_MAX, s // 4)
    qn = min(256, s)

    ia = ib = 0
    for i in range(depth):
        xa = (i, g_xattn, w_xq, kv_all, w_xo)
        if i % 2 == 0:
            x = _pool_xattn(x, i, g_mix, ia, w_pool, sc_pool, xa)
            ia += 1
        else:
            p = _dsa_proj(x, i, g_mix, _dsa_weights(attn_w_in[ib], d), tabs)
            a = jnp.zeros((b, s, d), BF16)
            for qb0 in range(s // qn):
                a = _dsa_attn(p, a, qb0=qb0, nqb=1, qn=qn, topk=topk)
            x = _proj_xattn(x, a, ib, w_attn_out, xa)
            ib += 1
        g_out = norm_final.reshape(1, d) if i == depth - 1 else None
        x = _ffn(x.reshape(b * s, d), i, g_ffn, w_ffn_in, w_ffn_out, g_out).reshape(b, s, d)
    return x
```

```python
import functools
import math

import jax
import jax.numpy as jnp
from jax import lax
from jax.experimental import pallas as pl
from jax.experimental.pallas import tpu as pltpu

EPS = 1e-6
ROPE_THETA = 10000.0
POOL_WINDOWS = (2, 4, 8, 16)
POOL_HALO = 16
HEAD_DIM = 128
IDX_HEADS = 16
IDX_DIM = 64
TOPK_MAX = 256
X_HEADS = 4
X_HEAD_DIM = 128

BF16 = jnp.bfloat16
F32 = jnp.float32
INT_MIN = -(2**31)
MASK_BIAS = -1e30
VMEM_LIMIT_BYTES = 56 * 1024 * 1024
SEQ_ROWS = 512
FFN_ROWS = 1024
FFN_COLS = 512

_NT = (((1,), (1,)), ((), ()))


def _cparams(*sem):
    return pltpu.CompilerParams(dimension_semantics=sem, vmem_limit_bytes=VMEM_LIMIT_BYTES)


def _rms(x, g):
    ms = jnp.mean(x * x, axis=-1, keepdims=True)
    return x * lax.rsqrt(ms + EPS) * g


def _dot(a, b):
    return jnp.dot(a, b, preferred_element_type=F32)


def _dot_nt(a, b):
    return lax.dot_general(a, b, _NT, preferred_element_type=F32)


def _const_spec(shape, layer=None):
    n = len(shape)
    if layer is None:
        return pl.BlockSpec(shape, lambda *_: (0,) * n, pipeline_mode=pl.Buffered(1))
    return pl.BlockSpec((None,) + tuple(shape), lambda *_: (layer,) + (0,) * n, pipeline_mode=pl.Buffered(1))


def _seq_tile(shape, seq_axis):
    def imap(bi, i):
        idx = [0] * len(shape)
        idx[0] = bi
        idx[seq_axis] = i
        return tuple(idx)
    return pl.BlockSpec(shape, imap)


def _memkv_kernel(mem_ref, g_ref, w_ref, o_ref):
    hn = _rms(mem_ref[...], g_ref[...]).astype(BF16)
    o_ref[...] = _dot(hn, w_ref[...]).astype(BF16)


def _memkv(mem2d, g, w_kv):
    rows, d = mem2d.shape
    depth, _, n = w_kv.shape
    tr = min(512, rows)
    return pl.pallas_call(
        _memkv_kernel,
        grid=(rows // tr, depth),
        in_specs=[
            pl.BlockSpec((tr, d), lambda i, l: (i, 0)),
            pl.BlockSpec((1, d), lambda i, l: (0, 0)),
            pl.BlockSpec((None, d, n), lambda i, l: (l, 0, 0)),
        ],
        out_specs=pl.BlockSpec((None, tr, n), lambda i, l: (l, i, 0)),
        out_shape=jax.ShapeDtypeStruct((depth, rows, n), BF16),
        compiler_params=_cparams("parallel", "arbitrary"),
        name="memkv",
    )(mem2d, g, w_kv)


def _xattn_apply(x, g_ref, wq_ref, k_ref, v_ref, wo_ref):
    hn = _rms(x, g_ref[...]).astype(BF16)
    q = _dot(hn, wq_ref[...])
    exp2_scale = X_HEAD_DIM ** -0.5 * math.log2(math.e)
    outs = []
    for h in range(X_HEADS):
        sl = slice(h * X_HEAD_DIM, (h + 1) * X_HEAD_DIM)
        logits = _dot_nt(q[:, sl].astype(BF16), k_ref[0, :, sl])
        m = jnp.max(logits, axis=-1, keepdims=True)
        e = jnp.exp2((logits - m) * exp2_scale)
        s = jnp.sum(e, axis=-1, keepdims=True)
        oh = _dot(e.astype(BF16), v_ref[0, :, sl]) / s
        outs.append(oh.astype(BF16))
    a = jnp.concatenate(outs, axis=-1)
    return x + _dot(a, wo_ref[...])


def _xattn_operands(xa, d):
    layer, g, wq, kv, wo = xa
    m = kv.shape[2]
    inner = X_HEADS * X_HEAD_DIM
    specs = [
        _const_spec((1, d), layer),
        _const_spec((d, inner), layer),
        pl.BlockSpec((None, 1, m, inner), lambda bi, i: (layer, bi, 0, 0)),
        pl.BlockSpec((None, 1, m, inner), lambda bi, i: (layer, bi, 0, 1)),
        _const_spec((inner, d), layer),
    ]
    return specs, [g, wq, kv, kv, wo]


def _ffn_kernel(x_ref, g_ref, wg_ref, wu_ref, wo_ref, *rest):
    o_ref, hn_ref = rest[-2:]
    j = pl.program_id(1)

    @pl.when(j == 0)
    def _():
        x = x_ref[...]
        hn_ref[...] = _rms(x, g_ref[...]).astype(BF16)
        o_ref[...] = x

    hn = hn_ref[...]
    gate = _dot(hn, wg_ref[...])
    up = _dot(hn, wu_ref[...])
    act = (gate / (1.0 + jnp.exp(-gate)) * up).astype(BF16)
    o_ref[...] += _dot(act, wo_ref[...])

    if len(rest) == 3:
        @pl.when(j == pl.num_programs(1) - 1)
        def _():
            o_ref[...] = _rms(o_ref[...], rest[0][...])


def _ffn(x2d, layer, g, w_in, w_out, g_out=None):
    rows, d = x2d.shape
    hidden = w_out.shape[1]
    tm = min(FFN_ROWS, rows)
    th = FFN_COLS if hidden % FFN_COLS == 0 else 256
    nh = hidden // th
    in_specs = [
        pl.BlockSpec((tm, d), lambda i, j: (i, 0)),
        _const_spec((1, d), layer),
        pl.BlockSpec((None, d, th), lambda i, j: (layer, 0, j)),
        pl.BlockSpec((None, d, th), lambda i, j: (layer, 0, nh + j)),
        pl.BlockSpec((None, th, d), lambda i, j: (layer, j, 0)),
    ]
    args = [x2d, g, w_in, w_in, w_out]
    if g_out is not None:
        in_specs.append(_const_spec((1, d)))
        args.append(g_out)
    return pl.pallas_call(
        _ffn_kernel,
        grid=(rows // tm, nh),
        in_specs=in_specs,
        out_specs=pl.BlockSpec((tm, d), lambda i, j: (i, 0)),
        out_shape=jax.ShapeDtypeStruct(x2d.shape, F32),
        scratch_shapes=[pltpu.VMEM((tm, d), BF16)],
        compiler_params=_cparams("parallel", "arbitrary"),
        name="ffn",
    )(*args)


def _pool_xattn_kernel(x_ref, halo_ref, g_ref, w_ref, sc_ref, gx_ref, wq_ref, k_ref, v_ref, wo_ref,
                       o_ref, hp_ref, x1_ref, *, ts):
    i = pl.program_id(1)
    x = x_ref[0]
    g = g_ref[...]
    hn = _rms(x, g)
    hh = _rms(halo_ref[0], g) * jnp.where(i > 0, 1.0, 0.0)
    hp_ref[0:POOL_HALO, :] = hh
    hp_ref[POOL_HALO:, :] = hn
    pos = i * ts + lax.broadcasted_iota(jnp.int32, (ts, 1), 0)
    gc = x.shape[-1] // len(POOL_WINDOWS)
    for gi, win in enumerate(POOL_WINDOWS):
        cols = slice(gi * gc, (gi + 1) * gc)
        acc = hp_ref[:, cols]
        step = 1
        while step < win:
            acc = acc + pltpu.roll(acc, step, axis=0)
            step *= 2
        cnt = jnp.minimum(pos + 1, win).astype(F32)
        mean = acc[POOL_HALO:, :] / cnt
        diff = (mean - hn[:, cols]).astype(BF16)
        y = _dot(diff, w_ref[gi])
        x1_ref[:, cols] = x[:, cols] + y * sc_ref[:, cols]
    o_ref[0] = _xattn_apply(x1_ref[...], gx_ref, wq_ref, k_ref, v_ref, wo_ref)


def _pool_xattn(x, layer, g, pool_layer, w, sc, xa):
    b, s, d = x.shape
    ts = min(SEQ_ROWS, s)
    hb = ts // POOL_HALO
    _, groups, gc, _ = w.shape
    xa_specs, xa_args = _xattn_operands(xa, d)
    return pl.pallas_call(
        functools.partial(_pool_xattn_kernel, ts=ts),
        grid=(b, s // ts),
        in_specs=[
            pl.BlockSpec((1, ts, d), lambda bi, i: (bi, i, 0)),
            pl.BlockSpec((1, POOL_HALO, d), lambda bi, i: (bi, jnp.maximum(i * hb - 1, 0), 0)),
            _const_spec((1, d), layer),
            _const_spec((groups, gc, gc), pool_layer),
            _const_spec((1, d), pool_layer),
        ] + xa_specs,
        out_specs=pl.BlockSpec((1, ts, d), lambda bi, i: (bi, i, 0)),
        out_shape=jax.ShapeDtypeStruct(x.shape, F32),
        scratch_shapes=[pltpu.VMEM((ts + POOL_HALO, d), F32), pltpu.VMEM((ts, d), F32)],
        compiler_params=_cparams("parallel", "arbitrary"),
        name="pool_xattn",
    )(x, x, g, w, sc, *xa_args)


def _dsa_proj_kernel(x_ref, g_ref, wq_ref, wkT_ref, wsm_ref, wqiT_ref, wwiT_ref,
                     cq_ref, sq_ref, caT_ref, saT_ref, ciT_ref, siT_ref, cci_ref, ssi_ref,
                     q_ref, kT_ref, v_ref, qiT_ref, ki_ref, wiT_ref, *, n_heads):
    hn = _rms(x_ref[0], g_ref[...]).astype(BF16)

    q = _dot(hn, wq_ref[...])
    cq = cq_ref[0]
    sq = sq_ref[0]
    for h in range(n_heads):
        qh = q[:, h * HEAD_DIM:(h + 1) * HEAD_DIM]
        q_ref[0, h] = (qh * cq + pltpu.roll(qh, HEAD_DIM // 2, axis=1) * sq).astype(BF16)

    kT = _dot_nt(wkT_ref[...], hn)
    half = HEAD_DIM // 2
    k1, k2 = kT[:half], kT[half:]
    ca, sa = caT_ref[0], saT_ref[0]
    kT_ref[0, :half, :] = (k1 * ca - k2 * sa).astype(BF16)
    kT_ref[0, half:, :] = (k2 * ca + k1 * sa).astype(BF16)

    sm = _dot(hn, wsm_ref[...])
    v_ref[0, :, :HEAD_DIM] = sm[:, :HEAD_DIM].astype(BF16)
    v_ref[0, :, HEAD_DIM:] = jnp.ones((sm.shape[0], HEAD_DIM), BF16)
    ki = sm[:, HEAD_DIM:HEAD_DIM + IDX_DIM] * cci_ref[0] + sm[:, HEAD_DIM + IDX_DIM:] * ssi_ref[0]
    ki_ref[0] = ki.astype(BF16)

    qiT = _dot_nt(wqiT_ref[...], hn)
    ci, si = ciT_ref[0], siT_ref[0]
    ih = IDX_DIM // 2
    for h in range(IDX_HEADS):
        r0 = h * IDX_DIM
        x1, x2 = qiT[r0:r0 + ih], qiT[r0 + ih:r0 + IDX_DIM]
        qiT_ref[0, r0:r0 + ih, :] = (x1 * ci - x2 * si).astype(BF16)
        qiT_ref[0, r0 + ih:r0 + IDX_DIM, :] = (x2 * ci + x1 * si).astype(BF16)

    wiT_ref[0] = _dot_nt(wwiT_ref[...], hn) * (IDX_HEADS ** -0.5 * IDX_DIM ** -0.5)


def _dsa_proj(x, layer, g, w, tabs):
    b, s, d = x.shape
    n_heads = d // HEAD_DIM
    ts = min(SEQ_ROWS, s)
    qi_rows = IDX_HEADS * IDX_DIM
    in_specs = [
        _seq_tile((1, ts, d), 1),
        _const_spec((1, d), layer),
        _const_spec(w["wq"].shape), _const_spec(w["wkT"].shape), _const_spec(w["wsm"].shape),
        _const_spec(w["wqiT"].shape), _const_spec(w["wwiT"].shape),
        _seq_tile((1, ts, HEAD_DIM), 1), _seq_tile((1, ts, HEAD_DIM), 1),
        _seq_tile((1, HEAD_DIM // 2, ts), 2), _seq_tile((1, HEAD_DIM // 2, ts), 2),
        _seq_tile((1, IDX_DIM // 2, ts), 2), _seq_tile((1, IDX_DIM // 2, ts), 2),
        _seq_tile((1, ts, IDX_DIM), 1), _seq_tile((1, ts, IDX_DIM), 1),
    ]
    out_specs = [
        _seq_tile((1, n_heads, ts, HEAD_DIM), 2),
        _seq_tile((1, HEAD_DIM, ts), 2),
        _seq_tile((1, ts, 2 * HEAD_DIM), 1),
        _seq_tile((1, qi_rows, ts), 2),
        _seq_tile((1, ts, IDX_DIM), 1),
        _seq_tile((1, IDX_HEADS, ts), 2),
    ]
    out_shape = [
        jax.ShapeDtypeStruct((b, n_heads, s, HEAD_DIM), BF16),
        jax.ShapeDtypeStruct((b, HEAD_DIM, s), BF16),
        jax.ShapeDtypeStruct((b, s, 2 * HEAD_DIM), BF16),
        jax.ShapeDtypeStruct((b, qi_rows, s), BF16),
        jax.ShapeDtypeStruct((b, s, IDX_DIM), BF16),
        jax.ShapeDtypeStruct((b, IDX_HEADS, s), F32),
    ]
    return pl.pallas_call(
        functools.partial(_dsa_proj_kernel, n_heads=n_heads),
        grid=(b, s // ts),
        in_specs=in_specs,
        out_specs=out_specs,
        out_shape=out_shape,
        compiler_params=_cparams("parallel", "arbitrary"),
        name="dsa_proj",
    )(x, g, w["wq"], w["wkT"], w["wsm"], w["wqiT"], w["wwiT"],
      tabs["cq"], tabs["sq"], tabs["caT"], tabs["saT"], tabs["ciT"], tabs["siT"], tabs["cci"], tabs["ssi"])


SCORE_ROWS = 128
SCORE_UNROLL = 2
TIE_ROWS = 256
PLANE_ROWS = 256
HEAD_GROUP = 2
ATT_ROWS = 128


def _bit_transpose32(words):
    words = list(words)
    j, m = 16, 0x0000FFFF
    while j:
        mask = jnp.int32(m)
        for k in range(32):
            if not k & j:
                t = (words[k] ^ (words[k + j] >> j)) & mask
                words[k] = words[k] ^ t
                words[k + j] = words[k + j] ^ (t << j)
        j >>= 1
        m = (m ^ (m << j)) & 0xFFFFFFFF
    return words


def _dsa_attn_kernel(qiT_ref, ki_ref, wiT_ref, q_ref, kT_ref, v_ref, prev_ref, o_ref,
                     keys_ref, planes_ref, biasT_ref, bias_ref, *, s_c, qn, topk, qb0, n_heads):
    del prev_ref
    q0 = (pl.program_id(1) + qb0) * qn
    t_pos = q0 + lax.broadcasted_iota(jnp.int32, (1, qn), 1)
    i_zero, i_one, i_min = jnp.int32(0), jnp.int32(1), jnp.int32(INT_MIN)
    f_zero, f_one, f_mask = jnp.float32(0.0), jnp.float32(1.0), jnp.float32(MASK_BIAS)

    def score_chunk(c, carry):
        r0 = pl.multiple_of(c * SCORE_ROWS, SCORE_ROWS)
        kic = ki_ref[0, pl.ds(r0, SCORE_ROWS), :]
        acc = jnp.zeros((SCORE_ROWS, qn), F32)
        for h in range(IDX_HEADS):
            rel = _dot(kic, qiT_ref[0, h * IDX_DIM:(h + 1) * IDX_DIM, :])
            acc = acc + wiT_ref[0, h:h + 1, :] * jnp.maximum(rel, 0.0)
        bits = pltpu.bitcast(acc + 0.0, jnp.int32)
        key = bits ^ ((bits >> 31) & jnp.int32(0x7FFFFFFF))
        s_pos = r0 + lax.broadcasted_iota(jnp.int32, (SCORE_ROWS, 1), 0)
        keys_ref[pl.ds(r0, SCORE_ROWS), :] = jnp.where(s_pos <= t_pos, key, i_min)
        return carry

    lax.fori_loop(0, s_c // SCORE_ROWS, score_chunk, 0, unroll=SCORE_UNROLL)

    def plane_block(bi, carry):
        r0 = bi * PLANE_ROWS
        p0 = pl.multiple_of(bi * 8, 8)
        for lt in range(qn // 128):
            lanes = slice(lt * 128, (lt + 1) * 128)
            words = [keys_ref[pl.ds(pl.multiple_of(r0 + 8 * j, 8), 8), lanes] ^ i_min for j in range(32)]
            for b, plane in enumerate(_bit_transpose32(words)):
                planes_ref[b, pl.ds(p0, 8), lanes] = plane
        return carry

    lax.fori_loop(0, s_c // PLANE_ROWS, plane_block, 0)

    kk = jnp.minimum(t_pos + 1, topk)

    def popcount_rows(words):
        return jnp.sum(lax.population_count(words), axis=0, keepdims=True)

    def refine(b, carry):
        alive, need, ans = carry
        ones = alive & planes_ref[b]
        cnt = popcount_rows(ones)
        take = cnt >= need
        alive = jnp.where(take, ones, alive ^ ones)
        need = jnp.where(take, need, need - cnt)
        ans = jnp.where(take, ans | jnp.left_shift(i_one, 31 - b), ans)
        return alive, need, ans

    alive = jnp.full((s_c // 32, qn), -1, jnp.int32)
    alive, need, ans = lax.fori_loop(0, 32, refine, (alive, kk, jnp.zeros((1, qn), jnp.int32)))
    ans = ans ^ i_min

    biasT_ref[...] = jnp.where(keys_ref[...] >= ans, f_zero, f_mask)
    tie = jnp.max(jnp.where(popcount_rows(alive) > need, i_one, i_zero))

    @pl.when(tie > 0)
    def _():
        needf = need.astype(F32)
        rr = lax.broadcasted_iota(jnp.int32, (TIE_ROWS, TIE_ROWS), 0)
        cc = lax.broadcasted_iota(jnp.int32, (TIE_ROWS, TIE_ROWS), 1)
        tri = jnp.where(cc < rr, f_one, f_zero).astype(BF16)
        carry = jnp.zeros((1, qn), F32)
        for c in range(s_c // TIE_ROWS):
            rows = slice(c * TIE_ROWS, (c + 1) * TIE_ROWS)
            kc = keys_ref[rows, :]
            eq = jnp.where(kc == ans, f_one, f_zero)
            before = _dot(tri, eq.astype(BF16)) + carry
            keep = jnp.where(kc > ans, f_one, eq * jnp.where(before < needf, f_one, f_zero))
            biasT_ref[rows, :] = jnp.where(keep > 0.5, f_zero, f_mask)
            carry = carry + jnp.sum(eq, axis=0, keepdims=True)

    for c in range(s_c // qn):
        cols = slice(c * qn, (c + 1) * qn)
        bias_ref[:, cols] = biasT_ref[cols, :].T

    kT = kT_ref[0]
    v1 = v_ref[0]
    hg = HEAD_GROUP
    tiles = [(g0, r0) for g0 in range(0, n_heads, hg) for r0 in range(0, qn, ATT_ROWS)]

    def qk(tile):
        g0, r0 = tile
        qh = q_ref[0, g0:g0 + hg, r0:r0 + ATT_ROWS, :].reshape(hg * ATT_ROWS, HEAD_DIM)
        return _dot(qh, kT).reshape(hg, ATT_ROWS, s_c) + bias_ref[r0:r0 + ATT_ROWS, :][None]

    logits_next = qk(tiles[0])
    for ti, (g0, r0) in enumerate(tiles):
        logits = logits_next
        if ti + 1 < len(tiles):
            logits_next = qk(tiles[ti + 1])
        m = jnp.max(logits, axis=-1, keepdims=True)
        e = jnp.exp2(logits - m)
        o = _dot(e.astype(BF16).reshape(hg * ATT_ROWS, s_c), v1).reshape(hg, ATT_ROWS, 2 * HEAD_DIM)
        o = o[:, :, :HEAD_DIM] / o[:, :, HEAD_DIM:]
        for j in range(hg):
            o_ref[0, r0:r0 + ATT_ROWS, (g0 + j) * HEAD_DIM:(g0 + j + 1) * HEAD_DIM] = o[j].astype(BF16)


def _dsa_attn(p, prev, *, qb0, nqb, qn, topk):
    q, kT, v, qiT, ki, wiT = p
    b, n_heads, s, _ = q.shape
    s_c = (qb0 + nqb) * qn
    d = n_heads * HEAD_DIM
    qi_rows = IDX_HEADS * IDX_DIM
    kern = functools.partial(_dsa_attn_kernel, s_c=s_c, qn=qn, topk=topk, qb0=qb0, n_heads=n_heads)
    in_specs = [
        pl.BlockSpec((1, qi_rows, qn), lambda bi, i: (bi, 0, qb0 + i)),
        pl.BlockSpec((1, s_c, IDX_DIM), lambda bi, i: (bi, 0, 0)),
        pl.BlockSpec((1, IDX_HEADS, qn), lambda bi, i: (bi, 0, qb0 + i)),
        pl.BlockSpec((1, n_heads, qn, HEAD_DIM), lambda bi, i: (bi, 0, qb0 + i, 0)),
        pl.BlockSpec((1, HEAD_DIM, s_c), lambda bi, i: (bi, 0, 0)),
        pl.BlockSpec((1, s_c, 2 * HEAD_DIM), lambda bi, i: (bi, 0, 0)),
        pl.BlockSpec(memory_space=pl.ANY),
    ]
    return pl.pallas_call(
        kern,
        grid=(b, nqb),
        in_specs=in_specs,
        out_specs=pl.BlockSpec((1, qn, d), lambda bi, i: (bi, qb0 + i, 0)),
        out_shape=jax.ShapeDtypeStruct((b, s, d), BF16),
        scratch_shapes=[
            pltpu.VMEM((s_c, qn), jnp.int32),
            pltpu.VMEM((32, s_c // 32, qn), jnp.int32),
            pltpu.VMEM((s_c, qn), F32),
            pltpu.VMEM((qn, s_c), F32),
        ],
        input_output_aliases={6: 0},
        compiler_params=_cparams("parallel", "arbitrary"),
        name=f"dsa_attn_{s_c}",
    )(qiT, ki, wiT, q, kT, v, prev)


def _proj_xattn_kernel(x_ref, a_ref, w_ref, gx_ref, wq_ref, k_ref, v_ref, wo_ref, o_ref):
    x1 = x_ref[0] + _dot(a_ref[0], w_ref[...])
    o_ref[0] = _xattn_apply(x1, gx_ref, wq_ref, k_ref, v_ref, wo_ref)


def _proj_xattn(x, a, attn_layer, w, xa):
    b, s, d = x.shape
    k = a.shape[-1]
    ts = min(SEQ_ROWS, s)
    xa_specs, xa_args = _xattn_operands(xa, d)
    return pl.pallas_call(
        _proj_xattn_kernel,
        grid=(b, s // ts),
        in_specs=[
            pl.BlockSpec((1, ts, d), lambda bi, i: (bi, i, 0)),
            pl.BlockSpec((1, ts, k), lambda bi, i: (bi, i, 0)),
            _const_spec((k, d), attn_layer),
        ] + xa_specs,
        out_specs=pl.BlockSpec((1, ts, d), lambda bi, i: (bi, i, 0)),
        out_shape=jax.ShapeDtypeStruct(x.shape, F32),
        compiler_params=_cparams("parallel", "arbitrary"),
        name="proj_xattn",
    )(x, a, w, *xa_args)


def _rope_tables(positions):
    def cs(dim):
        inv_freq = ROPE_THETA ** (-(jnp.arange(0, dim, 2, dtype=F32) / dim))
        ang = positions.astype(F32)[..., None] * inv_freq
        return jnp.cos(ang), jnp.sin(ang)

    cos_a, sin_a, cos_i, sin_i = lax.optimization_barrier(cs(HEAD_DIM) + cs(IDX_DIM))
    tr = lambda t: jnp.swapaxes(t, 1, 2)
    qs = HEAD_DIM ** -0.5 * math.log2(math.e)
    return {
        "cq": qs * jnp.concatenate([cos_a, cos_a], -1), "sq": qs * jnp.concatenate([-sin_a, sin_a], -1),
        "caT": tr(cos_a), "saT": tr(sin_a),
        "ciT": tr(cos_i), "siT": tr(sin_i),
        "cci": jnp.concatenate([cos_i, cos_i], -1), "ssi": jnp.concatenate([sin_i, sin_i], -1),
    }


def _dsa_weights(w_in, d):
    n_q = d
    o = 0
    wq = w_in[:, o:o + n_q]; o += n_q
    wk = w_in[:, o:o + HEAD_DIM]; o += HEAD_DIM
    wv = w_in[:, o:o + HEAD_DIM]; o += HEAD_DIM
    wqi = w_in[:, o:o + IDX_HEADS * IDX_DIM]; o += IDX_HEADS * IDX_DIM
    wki = w_in[:, o:o + IDX_DIM]; o += IDX_DIM
    wwi = w_in[:, o:o + IDX_HEADS]
    ih = IDX_DIM // 2
    wsm = jnp.concatenate([wv, wki, -wki[:, ih:], wki[:, :ih]], axis=1)
    c = lambda t: t.astype(BF16)
    return {"wq": c(wq), "wkT": c(wk.T), "wsm": c(wsm), "wqiT": c(wqi.T), "wwiT": c(wwi.T)}


def kernel(x, mem, positions, norm_mix, norm_xattn, norm_ffn, norm_memory, norm_final, pool_w, pool_scale,
           attn_w_in, attn_w_out, xattn_w_q, xattn_w_kv, xattn_w_o, ffn_w_in, ffn_w_out):
    b, s, d = x.shape
    depth = norm_mix.shape[0]
    m = mem.shape[1]
    rows = lambda t: t.reshape(-1, 1, d)
    c = lambda t: t.astype(BF16)

    tabs = _rope_tables(positions)
    kv_all = _memkv(mem.reshape(b * m, d), norm_memory.reshape(1, d), c(xattn_w_kv)).reshape(depth, b, m, -1)
    g_mix, g_xattn, g_ffn = rows(norm_mix), rows(norm_xattn), rows(norm_ffn)
    w_pool, sc_pool = c(pool_w), rows(pool_scale)
    w_attn_out = c(attn_w_out)
    w_xq, w_xo = c(xattn_w_q), c(xattn_w_o)
    w_ffn_in, w_ffn_out = c(ffn_w_in), c(ffn_w_out)

    topk = min(TOPK_MAX, s // 4)
    qn = min(256, s)

    a = jnp.zeros((b, s, d), BF16)
    ia = ib = 0
    for i in range(depth):
        xa = (i, g_xattn, w_xq, kv_all, w_xo)
        if i % 2 == 0:
            x = _pool_xattn(x, i, g_mix, ia, w_pool, sc_pool, xa)
            ia += 1
        else:
            p = _dsa_proj(x, i, g_mix, _dsa_weights(attn_w_in[ib], d), tabs)
            for qb0 in range(s // qn):
                a = _dsa_attn(p, a, qb0=qb0, nqb=1, qn=qn, topk=topk)
            x = _proj_xattn(x, a, ib, w_attn_out, xa)
            ib += 1
        g_out = norm_final.reshape(1, d) if i == depth - 1 else None
        x = _ffn(x.reshape(b * s, d), i, g_ffn, w_ffn_in, w_ffn_out, g_out).reshape(b, s, d)
    return x
```

```python
import functools
import math

import jax
import jax.numpy as jnp
from jax import lax
from jax.experimental import pallas as pl
from jax.experimental.pallas import tpu as pltpu

EPS = 1e-6
ROPE_THETA = 10000.0
POOL_WINDOWS = (2, 4, 8, 16)
POOL_HALO = 16
HEAD_DIM = 128
IDX_HEADS = 16
IDX_DIM = 64
TOPK_MAX = 256
X_HEADS = 4
X_HEAD_DIM = 128

BF16 = jnp.bfloat16
F32 = jnp.float32
INT_MIN = -(2**31)
MASK_BIAS = -1e30
VMEM_LIMIT_BYTES = 56 * 1024 * 1024
FFN_VMEM_LIMIT_BYTES = 60 * 1024 * 1024
SEQ_ROWS = 512
FFN_ROWS = 1024
FFN_COLS = 512

_NT = (((1,), (1,)), ((), ()))


def _cparams(*sem, vmem_limit_bytes=VMEM_LIMIT_BYTES):
    return pltpu.CompilerParams(dimension_semantics=sem, vmem_limit_bytes=vmem_limit_bytes)


def _rms(x, g):
    ms = jnp.mean(x * x, axis=-1, keepdims=True)
    return x * lax.rsqrt(ms + EPS) * g


def _dot(a, b):
    return jnp.dot(a, b, preferred_element_type=F32)


def _dot_nt(a, b):
    return lax.dot_general(a, b, _NT, preferred_element_type=F32)


def _const_spec(shape, layer=None):
    n = len(shape)
    if layer is None:
        return pl.BlockSpec(shape, lambda *_: (0,) * n, pipeline_mode=pl.Buffered(1))
    return pl.BlockSpec((None,) + tuple(shape), lambda *_: (layer,) + (0,) * n, pipeline_mode=pl.Buffered(1))


def _seq_tile(shape, seq_axis):
    def imap(bi, i):
        idx = [0] * len(shape)
        idx[0] = bi
        idx[seq_axis] = i
        return tuple(idx)
    return pl.BlockSpec(shape, imap)


def _memkv_kernel(mem_ref, g_ref, w_ref, o_ref):
    hn = _rms(mem_ref[...], g_ref[...]).astype(BF16)
    o_ref[...] = _dot(hn, w_ref[...]).astype(BF16)


def _memkv(mem2d, g, w_kv):
    rows, d = mem2d.shape
    depth, _, n = w_kv.shape
    tr = min(512, rows)
    return pl.pallas_call(
        _memkv_kernel,
        grid=(rows // tr, depth),
        in_specs=[
            pl.BlockSpec((tr, d), lambda i, l: (i, 0)),
            pl.BlockSpec((1, d), lambda i, l: (0, 0)),
            pl.BlockSpec((None, d, n), lambda i, l: (l, 0, 0)),
        ],
        out_specs=pl.BlockSpec((None, tr, n), lambda i, l: (l, i, 0)),
        out_shape=jax.ShapeDtypeStruct((depth, rows, n), BF16),
        compiler_params=_cparams("parallel", "arbitrary"),
        name="memkv",
    )(mem2d, g, w_kv)


def _xattn_apply(x, g_ref, wq_ref, k_ref, v_ref, wo_ref):
    hn = _rms(x, g_ref[...]).astype(BF16)
    q = _dot(hn, wq_ref[...])
    exp2_scale = X_HEAD_DIM ** -0.5 * math.log2(math.e)
    outs = []
    for h in range(X_HEADS):
        sl = slice(h * X_HEAD_DIM, (h + 1) * X_HEAD_DIM)
        logits = _dot_nt(q[:, sl].astype(BF16), k_ref[0, :, sl])
        m = jnp.max(logits, axis=-1, keepdims=True)
        e = jnp.exp2((logits - m) * exp2_scale)
        s = jnp.sum(e, axis=-1, keepdims=True)
        oh = _dot(e.astype(BF16), v_ref[0, :, sl]) / s
        outs.append(oh.astype(BF16))
    a = jnp.concatenate(outs, axis=-1)
    return x + _dot(a, wo_ref[...])


def _xattn_operands(xa, d):
    layer, g, wq, kv, wo = xa
    m = kv.shape[2]
    inner = X_HEADS * X_HEAD_DIM
    specs = [
        _const_spec((1, d), layer),
        _const_spec((d, inner), layer),
        pl.BlockSpec((None, 1, m, inner), lambda bi, i: (layer, bi, 0, 0)),
        pl.BlockSpec((None, 1, m, inner), lambda bi, i: (layer, bi, 0, 1)),
        _const_spec((inner, d), layer),
    ]
    return specs, [g, wq, kv, kv, wo]


def _ffn_kernel(x_ref, g_ref, wg_ref, wu_ref, wo_ref, *rest):
    o_ref, hn_ref = rest[-2:]
    j = pl.program_id(1)

    @pl.when(j == 0)
    def _():
        x = x_ref[...]
        hn_ref[...] = _rms(x, g_ref[...]).astype(BF16)
        o_ref[...] = x

    hn = hn_ref[...]
    gate = _dot(hn, wg_ref[...])
    up = _dot(hn, wu_ref[...])
    act = (gate / (1.0 + jnp.exp(-gate)) * up).astype(BF16)
    o_ref[...] += _dot(act, wo_ref[...].astype(BF16))

    if len(rest) == 3:
        @pl.when(j == pl.num_programs(1) - 1)
        def _():
            o_ref[...] = _rms(o_ref[...], rest[0][...])


def _ffn(x2d, layer, g, w_in, w_out, g_out=None):
    rows, d = x2d.shape
    hidden = w_out.shape[1]
    tm = min(FFN_ROWS, rows)
    th = FFN_COLS if hidden % FFN_COLS == 0 else 256
    nh = hidden // th
    in_specs = [
        pl.BlockSpec((tm, d), lambda i, j: (i, 0)),
        _const_spec((1, d), layer),
        pl.BlockSpec((None, d, th), lambda i, j: (layer, 0, j)),
        pl.BlockSpec((None, d, th), lambda i, j: (layer, 0, nh + j)),
        pl.BlockSpec((None, th, d), lambda i, j: (layer, j, 0)),
    ]
    args = [x2d, g, w_in, w_in, w_out]
    if g_out is not None:
        in_specs.append(_const_spec((1, d)))
        args.append(g_out)
    return pl.pallas_call(
        _ffn_kernel,
        grid=(rows // tm, nh),
        in_specs=in_specs,
        out_specs=pl.BlockSpec((tm, d), lambda i, j: (i, 0)),
        out_shape=jax.ShapeDtypeStruct(x2d.shape, F32),
        scratch_shapes=[pltpu.VMEM((tm, d), BF16)],
        compiler_params=_cparams("parallel", "arbitrary", vmem_limit_bytes=FFN_VMEM_LIMIT_BYTES),
        name="ffn",
    )(*args)


def _pool_xattn_kernel(x_ref, halo_ref, g_ref, w_ref, sc_ref, gx_ref, wq_ref, k_ref, v_ref, wo_ref,
                       o_ref, hp_ref, x1_ref, *, ts):
    i = pl.program_id(1)
    x = x_ref[0]
    g = g_ref[...]
    hn = _rms(x, g)
    hh = _rms(halo_ref[0], g) * jnp.where(i > 0, 1.0, 0.0)
    hp_ref[0:POOL_HALO, :] = hh
    hp_ref[POOL_HALO:, :] = hn
    pos = i * ts + lax.broadcasted_iota(jnp.int32, (ts, 1), 0)
    gc = x.shape[-1] // len(POOL_WINDOWS)
    for gi, win in enumerate(POOL_WINDOWS):
        cols = slice(gi * gc, (gi + 1) * gc)
        acc = hp_ref[:, cols]
        step = 1
        while step < win:
            acc = acc + pltpu.roll(acc, step, axis=0)
            step *= 2
        cnt = jnp.minimum(pos + 1, win).astype(F32)
        mean = acc[POOL_HALO:, :] / cnt
        diff = (mean - hn[:, cols]).astype(BF16)
        y = _dot(diff, w_ref[gi])
        x1_ref[:, cols] = x[:, cols] + y * sc_ref[:, cols]
    o_ref[0] = _xattn_apply(x1_ref[...], gx_ref, wq_ref, k_ref, v_ref, wo_ref)


def _pool_xattn(x, layer, g, pool_layer, w, sc, xa):
    b, s, d = x.shape
    ts = min(SEQ_ROWS, s)
    hb = ts // POOL_HALO
    _, groups, gc, _ = w.shape
    xa_specs, xa_args = _xattn_operands(xa, d)
    return pl.pallas_call(
        functools.partial(_pool_xattn_kernel, ts=ts),
        grid=(b, s // ts),
        in_specs=[
            pl.BlockSpec((1, ts, d), lambda bi, i: (bi, i, 0)),
            pl.BlockSpec((1, POOL_HALO, d), lambda bi, i: (bi, jnp.maximum(i * hb - 1, 0), 0)),
            _const_spec((1, d), layer),
            _const_spec((groups, gc, gc), pool_layer),
            _const_spec((1, d), pool_layer),
        ] + xa_specs,
        out_specs=pl.BlockSpec((1, ts, d), lambda bi, i: (bi, i, 0)),
        out_shape=jax.ShapeDtypeStruct(x.shape, F32),
        scratch_shapes=[pltpu.VMEM((ts + POOL_HALO, d), F32), pltpu.VMEM((ts, d), F32)],
        compiler_params=_cparams("parallel", "arbitrary"),
        name="pool_xattn",
    )(x, x, g, w, sc, *xa_args)


def _dsa_proj_kernel(x_ref, g_ref, wq_ref, wkT_ref, wsm_ref, wqiT_ref, wwiT_ref,
                     cq_ref, sq_ref, caT_ref, saT_ref, ciT_ref, siT_ref, cci_ref, ssi_ref,
                     q_ref, kT_ref, v_ref, qiT_ref, ki_ref, wiT_ref, *, n_heads):
    hn = _rms(x_ref[0], g_ref[...]).astype(BF16)

    q = _dot(hn, wq_ref[...])
    cq = cq_ref[0]
    sq = sq_ref[0]
    for h in range(n_heads):
        qh = q[:, h * HEAD_DIM:(h + 1) * HEAD_DIM]
        q_ref[0, h] = (qh * cq + pltpu.roll(qh, HEAD_DIM // 2, axis=1) * sq).astype(BF16)

    kT = _dot_nt(wkT_ref[...], hn)
    half = HEAD_DIM // 2
    k1, k2 = kT[:half], kT[half:]
    ca, sa = caT_ref[0], saT_ref[0]
    kT_ref[0, :half, :] = (k1 * ca - k2 * sa).astype(BF16)
    kT_ref[0, half:, :] = (k2 * ca + k1 * sa).astype(BF16)

    sm = _dot(hn, wsm_ref[...])
    v_ref[0, :, :HEAD_DIM] = sm[:, :HEAD_DIM].astype(BF16)
    v_ref[0, :, HEAD_DIM:] = jnp.ones((sm.shape[0], HEAD_DIM), BF16)
    ki = sm[:, HEAD_DIM:HEAD_DIM + IDX_DIM] * cci_ref[0] + sm[:, HEAD_DIM + IDX_DIM:] * ssi_ref[0]
    ki_ref[0] = ki.astype(BF16)

    qiT = _dot_nt(wqiT_ref[...], hn)
    ci, si = ciT_ref[0], siT_ref[0]
    ih = IDX_DIM // 2
    for h in range(IDX_HEADS):
        r0 = h * IDX_DIM
        x1, x2 = qiT[r0:r0 + ih], qiT[r0 + ih:r0 + IDX_DIM]
        qiT_ref[0, r0:r0 + ih, :] = (x1 * ci - x2 * si).astype(BF16)
        qiT_ref[0, r0 + ih:r0 + IDX_DIM, :] = (x2 * ci + x1 * si).astype(BF16)

    wiT_ref[0] = _dot_nt(wwiT_ref[...], hn) * (IDX_HEADS ** -0.5 * IDX_DIM ** -0.5)


def _dsa_proj(x, layer, g, w, tabs):
    b, s, d = x.shape
    n_heads = d // HEAD_DIM
    ts = min(SEQ_ROWS, s)
    qi_rows = IDX_HEADS * IDX_DIM
    in_specs = [
        _seq_tile((1, ts, d), 1),
        _const_spec((1, d), layer),
        _const_spec(w["wq"].shape), _const_spec(w["wkT"].shape), _const_spec(w["wsm"].shape),
        _const_spec(w["wqiT"].shape), _const_spec(w["wwiT"].shape),
        _seq_tile((1, ts, HEAD_DIM), 1), _seq_tile((1, ts, HEAD_DIM), 1),
        _seq_tile((1, HEAD_DIM // 2, ts), 2), _seq_tile((1, HEAD_DIM // 2, ts), 2),
        _seq_tile((1, IDX_DIM // 2, ts), 2), _seq_tile((1, IDX_DIM // 2, ts), 2),
        _seq_tile((1, ts, IDX_DIM), 1), _seq_tile((1, ts, IDX_DIM), 1),
    ]
    out_specs = [
        _seq_tile((1, n_heads, ts, HEAD_DIM), 2),
        _seq_tile((1, HEAD_DIM, ts), 2),
        _seq_tile((1, ts, 2 * HEAD_DIM), 1),
        _seq_tile((1, qi_rows, ts), 2),
        _seq_tile((1, ts, IDX_DIM), 1),
        _seq_tile((1, IDX_HEADS, ts), 2),
    ]
    out_shape = [
        jax.ShapeDtypeStruct((b, n_heads, s, HEAD_DIM), BF16),
        jax.ShapeDtypeStruct((b, HEAD_DIM, s), BF16),
        jax.ShapeDtypeStruct((b, s, 2 * HEAD_DIM), BF16),
        jax.ShapeDtypeStruct((b, qi_rows, s), BF16),
        jax.ShapeDtypeStruct((b, s, IDX_DIM), BF16),
        jax.ShapeDtypeStruct((b, IDX_HEADS, s), F32),
    ]
    return pl.pallas_call(
        functools.partial(_dsa_proj_kernel, n_heads=n_heads),
        grid=(b, s // ts),
        in_specs=in_specs,
        out_specs=out_specs,
        out_shape=out_shape,
        compiler_params=_cparams("parallel", "arbitrary"),
        name="dsa_proj",
    )(x, g, w["wq"], w["wkT"], w["wsm"], w["wqiT"], w["wwiT"],
      tabs["cq"], tabs["sq"], tabs["caT"], tabs["saT"], tabs["ciT"], tabs["siT"], tabs["cci"], tabs["ssi"])


SCORE_ROWS = 128
SCORE_UNROLL = 8
TIE_ROWS = 256
PLANE_ROWS = 256
HEAD_GROUP = 2
ATT_ROWS = 128


def _bit_transpose32(words):
    words = list(words)
    j, m = 16, 0x0000FFFF
    while j:
        mask = jnp.int32(m)
        for k in range(32):
            if not k & j:
                t = (words[k] ^ (words[k + j] >> j)) & mask
                words[k] = words[k] ^ t
                words[k + j] = words[k + j] ^ (t << j)
        j >>= 1
        m = (m ^ (m << j)) & 0xFFFFFFFF
    return words


def _dsa_attn_kernel(qiT_ref, ki_ref, wiT_ref, q_ref, kT_ref, v_ref, prev_ref, o_ref,
                     keys_ref, planes_ref, biasT_ref, bias_ref, *, s_c, qn, topk, qb0, n_heads):
    del prev_ref
    q0 = (pl.program_id(1) + qb0) * qn
    t_pos = q0 + lax.broadcasted_iota(jnp.int32, (1, qn), 1)
    i_zero, i_one, i_min = jnp.int32(0), jnp.int32(1), jnp.int32(INT_MIN)
    f_zero, f_one, f_mask = jnp.float32(0.0), jnp.float32(1.0), jnp.float32(MASK_BIAS)

    def score_chunk(c, carry):
        r0 = pl.multiple_of(c * SCORE_ROWS, SCORE_ROWS)
        kic = ki_ref[0, pl.ds(r0, SCORE_ROWS), :]
        acc = jnp.zeros((SCORE_ROWS, qn), F32)
        for h in range(IDX_HEADS):
            rel = _dot(kic, qiT_ref[0, h * IDX_DIM:(h + 1) * IDX_DIM, :])
            acc = acc + wiT_ref[0, h:h + 1, :] * jnp.maximum(rel, 0.0)
        bits = pltpu.bitcast(acc + 0.0, jnp.int32)
        key = bits ^ ((bits >> 31) & jnp.int32(0x7FFFFFFF))
        s_pos = r0 + lax.broadcasted_iota(jnp.int32, (SCORE_ROWS, 1), 0)
        keys_ref[pl.ds(r0, SCORE_ROWS), :] = jnp.where(s_pos <= t_pos, key, i_min)
        return carry

    n_chunks = s_c // SCORE_ROWS
    unroll = max(u for u in range(1, SCORE_UNROLL + 1) if n_chunks % u == 0)
    lax.fori_loop(0, n_chunks, score_chunk, 0, unroll=unroll)

    def plane_block(bi, carry):
        r0 = bi * PLANE_ROWS
        p0 = pl.multiple_of(bi * 8, 8)
        for lt in range(qn // 128):
            lanes = slice(lt * 128, (lt + 1) * 128)
            words = [keys_ref[pl.ds(pl.multiple_of(r0 + 8 * j, 8), 8), lanes] ^ i_min for j in range(32)]
            for b, plane in enumerate(_bit_transpose32(words)):
                planes_ref[b, pl.ds(p0, 8), lanes] = plane
        return carry

    lax.fori_loop(0, s_c // PLANE_ROWS, plane_block, 0)

    kk = jnp.minimum(t_pos + 1, topk)

    def popcount_rows(words):
        return jnp.sum(lax.population_count(words), axis=0, keepdims=True)

    def refine(b, carry):
        alive, need, ans = carry
        ones = alive & planes_ref[b]
        cnt = popcount_rows(ones)
        take = cnt >= need
        alive = jnp.where(take, ones, alive ^ ones)
        need = jnp.where(take, need, need - cnt)
        ans = jnp.where(take, ans | jnp.left_shift(i_one, 31 - b), ans)
        return alive, need, ans

    alive = jnp.full((s_c // 32, qn), -1, jnp.int32)
    alive, need, ans = lax.fori_loop(0, 32, refine, (alive, kk, jnp.zeros((1, qn), jnp.int32)))
    ans = ans ^ i_min

    biasT_ref[...] = jnp.where(keys_ref[...] >= ans, f_zero, f_mask)
    tie = jnp.max(jnp.where(popcount_rows(alive) > need, i_one, i_zero))

    @pl.when(tie > 0)
    def _():
        needf = need.astype(F32)
        rr = lax.broadcasted_iota(jnp.int32, (TIE_ROWS, TIE_ROWS), 0)
        cc = lax.broadcasted_iota(jnp.int32, (TIE_ROWS, TIE_ROWS), 1)
        tri = jnp.where(cc < rr, f_one, f_zero).astype(BF16)
        carry = jnp.zeros((1, qn), F32)
        for c in range(s_c // TIE_ROWS):
            rows = slice(c * TIE_ROWS, (c + 1) * TIE_ROWS)
            kc = keys_ref[rows, :]
            eq = jnp.where(kc == ans, f_one, f_zero)
            before = _dot(tri, eq.astype(BF16)) + carry
            keep = jnp.where(kc > ans, f_one, eq * jnp.where(before < needf, f_one, f_zero))
            biasT_ref[rows, :] = jnp.where(keep > 0.5, f_zero, f_mask)
            carry = carry + jnp.sum(eq, axis=0, keepdims=True)

    for c in range(s_c // qn):
        cols = slice(c * qn, (c + 1) * qn)
        bias_ref[:, cols] = biasT_ref[cols, :].T

    kT = kT_ref[0]
    v1 = v_ref[0]
    hg = HEAD_GROUP
    tiles = [(g0, r0) for g0 in range(0, n_heads, hg) for r0 in range(0, qn, ATT_ROWS)]

    def qk(tile):
        g0, r0 = tile
        qh = q_ref[0, g0:g0 + hg, r0:r0 + ATT_ROWS, :].reshape(hg * ATT_ROWS, HEAD_DIM)
        return _dot(qh, kT).reshape(hg, ATT_ROWS, s_c) + bias_ref[r0:r0 + ATT_ROWS, :][None]

    logits_next = qk(tiles[0])
    for ti, (g0, r0) in enumerate(tiles):
        logits = logits_next
        if ti + 1 < len(tiles):
            logits_next = qk(tiles[ti + 1])
        m = jnp.max(logits, axis=-1, keepdims=True)
        e = jnp.exp2(logits - m)
        o = _dot(e.astype(BF16).reshape(hg * ATT_ROWS, s_c), v1).reshape(hg, ATT_ROWS, 2 * HEAD_DIM)
        o = o[:, :, :HEAD_DIM] / o[:, :, HEAD_DIM:]
        for j in range(hg):
            o_ref[0, r0:r0 + ATT_ROWS, (g0 + j) * HEAD_DIM:(g0 + j + 1) * HEAD_DIM] = o[j].astype(BF16)


def _dsa_attn(p, prev, *, qb0, nqb, qn, topk):
    q, kT, v, qiT, ki, wiT = p
    b, n_heads, s, _ = q.shape
    s_c = (qb0 + nqb) * qn
    d = n_heads * HEAD_DIM
    qi_rows = IDX_HEADS * IDX_DIM
    kern = functools.partial(_dsa_attn_kernel, s_c=s_c, qn=qn, topk=topk, qb0=qb0, n_heads=n_heads)
    in_specs = [
        pl.BlockSpec((1, qi_rows, qn), lambda bi, i: (bi, 0, qb0 + i)),
        pl.BlockSpec((1, s_c, IDX_DIM), lambda bi, i: (bi, 0, 0)),
        pl.BlockSpec((1, IDX_HEADS, qn), lambda bi, i: (bi, 0, qb0 + i)),
        pl.BlockSpec((1, n_heads, qn, HEAD_DIM), lambda bi, i: (bi, 0, qb0 + i, 0)),
        pl.BlockSpec((1, HEAD_DIM, s_c), lambda bi, i: (bi, 0, 0)),
        pl.BlockSpec((1, s_c, 2 * HEAD_DIM), lambda bi, i: (bi, 0, 0)),
        pl.BlockSpec(memory_space=pl.ANY),
    ]
    return pl.pallas_call(
        kern,
        grid=(b, nqb),
        in_specs=in_specs,
        out_specs=pl.BlockSpec((1, qn, d), lambda bi, i: (bi, qb0 + i, 0)),
        out_shape=jax.ShapeDtypeStruct((b, s, d), BF16),
        scratch_shapes=[
            pltpu.VMEM((s_c, qn), jnp.int32),
            pltpu.VMEM((32, s_c // 32, qn), jnp.int32),
            pltpu.VMEM((s_c, qn), F32),
            pltpu.VMEM((qn, s_c), F32),
        ],
        input_output_aliases={6: 0},
        compiler_params=_cparams("parallel", "arbitrary"),
        name=f"dsa_attn_{s_c}",
    )(qiT, ki, wiT, q, kT, v, prev)


def _proj_xattn_kernel(x_ref, a_ref, w_ref, gx_ref, wq_ref, k_ref, v_ref, wo_ref, o_ref):
    x1 = x_ref[0] + _dot(a_ref[0], w_ref[...])
    o_ref[0] = _xattn_apply(x1, gx_ref, wq_ref, k_ref, v_ref, wo_ref)


def _proj_xattn(x, a, attn_layer, w, xa):
    b, s, d = x.shape
    k = a.shape[-1]
    ts = min(SEQ_ROWS, s)
    xa_specs, xa_args = _xattn_operands(xa, d)
    return pl.pallas_call(
        _proj_xattn_kernel,
        grid=(b, s // ts),
        in_specs=[
            pl.BlockSpec((1, ts, d), lambda bi, i: (bi, i, 0)),
            pl.BlockSpec((1, ts, k), lambda bi, i: (bi, i, 0)),
            _const_spec((k, d), attn_layer),
        ] + xa_specs,
        out_specs=pl.BlockSpec((1, ts, d), lambda bi, i: (bi, i, 0)),
        out_shape=jax.ShapeDtypeStruct(x.shape, F32),
        compiler_params=_cparams("parallel", "arbitrary"),
        name="proj_xattn",
    )(x, a, w, *xa_args)


def _rope_tables(positions):
    def cs(dim):
        inv_freq = ROPE_THETA ** (-(jnp.arange(0, dim, 2, dtype=F32) / dim))
        ang = positions.astype(F32)[..., None] * inv_freq
        return jnp.cos(ang), jnp.sin(ang)

    cos_a, sin_a, cos_i, sin_i = lax.optimization_barrier(cs(HEAD_DIM) + cs(IDX_DIM))
    tr = lambda t: jnp.swapaxes(t, 1, 2)
    qs = HEAD_DIM ** -0.5 * math.log2(math.e)
    return {
        "cq": qs * jnp.concatenate([cos_a, cos_a], -1), "sq": qs * jnp.concatenate([-sin_a, sin_a], -1),
        "caT": tr(cos_a), "saT": tr(sin_a),
        "ciT": tr(cos_i), "siT": tr(sin_i),
        "cci": jnp.concatenate([cos_i, cos_i], -1), "ssi": jnp.concatenate([sin_i, sin_i], -1),
    }


def _dsa_weights(w_in, d):
    n_q = d
    o = 0
    wq = w_in[:, o:o + n_q]; o += n_q
    wk = w_in[:, o:o + HEAD_DIM]; o += HEAD_DIM
    wv = w_in[:, o:o + HEAD_DIM]; o += HEAD_DIM
    wqi = w_in[:, o:o + IDX_HEADS * IDX_DIM]; o += IDX_HEADS * IDX_DIM
    wki = w_in[:, o:o + IDX_DIM]; o += IDX_DIM
    wwi = w_in[:, o:o + IDX_HEADS]
    ih = IDX_DIM // 2
    wsm = jnp.concatenate([wv, wki, -wki[:, ih:], wki[:, :ih]], axis=1)
    c = lambda t: t.astype(BF16)
    return {"wq": c(wq), "wkT": c(wk.T), "wsm": c(wsm), "wqiT": c(wqi.T), "wwiT": c(wwi.T)}


def kernel(x, mem, positions, norm_mix, norm_xattn, norm_ffn, norm_memory, norm_final, pool_w, pool_scale,
           attn_w_in, attn_w_out, xattn_w_q, xattn_w_kv, xattn_w_o, ffn_w_in, ffn_w_out):
    b, s, d = x.shape
    depth = norm_mix.shape[0]
    m = mem.shape[1]
    rows = lambda t: t.reshape(-1, 1, d)
    c = lambda t: t.astype(BF16)

    tabs = _rope_tables(positions)
    kv_all = _memkv(mem.reshape(b * m, d), norm_memory.reshape(1, d), c(xattn_w_kv)).reshape(depth, b, m, -1)
    g_mix, g_xattn, g_ffn = rows(norm_mix), rows(norm_xattn), rows(norm_ffn)
    w_pool, sc_pool = c(pool_w), rows(pool_scale)
    w_attn_out = c(attn_w_out)
    w_xq, w_xo = c(xattn_w_q), c(xattn_w_o)
    w_ffn_in = c(ffn_w_in)

    topk = min(TOPK_MAX, s // 4)
    qn = min(256, s)

    a = jnp.zeros((b, s, d), BF16)
    ia = ib = 0
    for i in range(depth):
        xa = (i, g_xattn, w_xq, kv_all, w_xo)
        if i % 2 == 0:
            x = _pool_xattn(x, i, g_mix, ia, w_pool, sc_pool, xa)
            ia += 1
        else:
            p = _dsa_proj(x, i, g_mix, _dsa_weights(attn_w_in[ib], d), tabs)
            for qb0 in range(s // qn):
                a = _dsa_attn(p, a, qb0=qb0, nqb=1, qn=qn, topk=topk)
            x = _proj_xattn(x, a, ib, w_attn_out, xa)
            ib += 1
        g_out = norm_final.reshape(1, d) if i == depth - 1 else None
        x = _ffn(x.reshape(b * s, d), i, g_ffn, w_ffn_in, ffn_w_out, g_out).reshape(b, s, d)
    return x
```

```python
import functools
import math

import jax
import jax.numpy as jnp
from jax import lax
from jax.experimental import pallas as pl
from jax.experimental.pallas import tpu as pltpu

EPS = 1e-6
ROPE_THETA = 10000.0
POOL_WINDOWS = (2, 4, 8, 16)
POOL_HALO = 16
HEAD_DIM = 128
IDX_HEADS = 16
IDX_DIM = 64
TOPK_MAX = 256
X_HEADS = 4
X_HEAD_DIM = 128

BF16 = jnp.bfloat16
F32 = jnp.float32
INT_MIN = -(2**31)
MASK_BIAS = -1e30
VMEM_LIMIT_BYTES = 56 * 1024 * 1024
FFN_VMEM_LIMIT_BYTES = 60 * 1024 * 1024
SEQ_ROWS = 512
FFN_ROWS = 1024
FFN_COLS = 512

_NT = (((1,), (1,)), ((), ()))


def _cparams(*sem, vmem_limit_bytes=VMEM_LIMIT_BYTES):
    return pltpu.CompilerParams(dimension_semantics=sem, vmem_limit_bytes=vmem_limit_bytes)


def _rms(x, g):
    ms = jnp.mean(x * x, axis=-1, keepdims=True)
    return x * lax.rsqrt(ms + EPS) * g


def _dot(a, b):
    return jnp.dot(a, b, preferred_element_type=F32)


def _dot_nt(a, b):
    return lax.dot_general(a, b, _NT, preferred_element_type=F32)


def _const_spec(shape, layer=None):
    n = len(shape)
    if layer is None:
        return pl.BlockSpec(shape, lambda *_: (0,) * n, pipeline_mode=pl.Buffered(1))
    return pl.BlockSpec((None,) + tuple(shape), lambda *_: (layer,) + (0,) * n, pipeline_mode=pl.Buffered(1))


def _seq_tile(shape, seq_axis):
    def imap(bi, i):
        idx = [0] * len(shape)
        idx[0] = bi
        idx[seq_axis] = i
        return tuple(idx)
    return pl.BlockSpec(shape, imap)


def _memkv_kernel(mem_ref, g_ref, w_ref, o_ref):
    hn = _rms(mem_ref[...], g_ref[...]).astype(BF16)
    o_ref[...] = _dot(hn, w_ref[...]).astype(BF16)


def _memkv(mem2d, g, w_kv):
    rows, d = mem2d.shape
    depth, _, n = w_kv.shape
    tr = min(512, rows)
    return pl.pallas_call(
        _memkv_kernel,
        grid=(rows // tr, depth),
        in_specs=[
            pl.BlockSpec((tr, d), lambda i, l: (i, 0)),
            pl.BlockSpec((1, d), lambda i, l: (0, 0)),
            pl.BlockSpec((None, d, n), lambda i, l: (l, 0, 0)),
        ],
        out_specs=pl.BlockSpec((None, tr, n), lambda i, l: (l, i, 0)),
        out_shape=jax.ShapeDtypeStruct((depth, rows, n), BF16),
        compiler_params=_cparams("parallel", "arbitrary"),
        name="memkv",
    )(mem2d, g, w_kv)


def _xattn_apply(x, g_ref, wq_ref, k_ref, v_ref, wo_ref):
    hn = _rms(x, g_ref[...]).astype(BF16)
    q = _dot(hn, wq_ref[...])
    exp2_scale = X_HEAD_DIM ** -0.5 * math.log2(math.e)
    outs = []
    for h in range(X_HEADS):
        sl = slice(h * X_HEAD_DIM, (h + 1) * X_HEAD_DIM)
        logits = _dot_nt(q[:, sl].astype(BF16), k_ref[0, :, sl])
        m = jnp.max(logits, axis=-1, keepdims=True)
        e = jnp.exp2((logits - m) * exp2_scale)
        s = jnp.sum(e, axis=-1, keepdims=True)
        oh = _dot(e.astype(BF16), v_ref[0, :, sl]) / s
        outs.append(oh.astype(BF16))
    a = jnp.concatenate(outs, axis=-1)
    return x + _dot(a, wo_ref[...])


def _xattn_operands(xa, d):
    layer, g, wq, kv, wo = xa
    m = kv.shape[2]
    inner = X_HEADS * X_HEAD_DIM
    specs = [
        _const_spec((1, d), layer),
        _const_spec((d, inner), layer),
        pl.BlockSpec((None, 1, m, inner), lambda bi, i: (layer, bi, 0, 0)),
        pl.BlockSpec((None, 1, m, inner), lambda bi, i: (layer, bi, 0, 1)),
        _const_spec((inner, d), layer),
    ]
    return specs, [g, wq, kv, kv, wo]


def _ffn_kernel(x_ref, g_ref, wg_ref, wu_ref, wo_ref, *rest):
    o_ref, hn_ref = rest[-2:]
    j = pl.program_id(1)

    @pl.when(j == 0)
    def _():
        x = x_ref[...]
        hn_ref[...] = _rms(x, g_ref[...]).astype(BF16)
        o_ref[...] = x

    hn = hn_ref[...]
    gate = _dot(hn, wg_ref[...])
    up = _dot(hn, wu_ref[...])
    act = (gate / (1.0 + jnp.exp(-gate)) * up).astype(BF16)
    o_ref[...] += _dot(act, wo_ref[...].astype(BF16))

    if len(rest) == 3:
        @pl.when(j == pl.num_programs(1) - 1)
        def _():
            o_ref[...] = _rms(o_ref[...], rest[0][...])


def _ffn(x2d, layer, g, w_in, w_out, g_out=None):
    rows, d = x2d.shape
    hidden = w_out.shape[1]
    tm = min(FFN_ROWS, rows)
    th = FFN_COLS if hidden % FFN_COLS == 0 else 256
    nh = hidden // th
    in_specs = [
        pl.BlockSpec((tm, d), lambda i, j: (i, 0)),
        _const_spec((1, d), layer),
        pl.BlockSpec((None, d, th), lambda i, j: (layer, 0, j)),
        pl.BlockSpec((None, d, th), lambda i, j: (layer, 0, nh + j)),
        pl.BlockSpec((None, th, d), lambda i, j: (layer, j, 0)),
    ]
    args = [x2d, g, w_in, w_in, w_out]
    if g_out is not None:
        in_specs.append(_const_spec((1, d)))
        args.append(g_out)
    return pl.pallas_call(
        _ffn_kernel,
        grid=(rows // tm, nh),
        in_specs=in_specs,
        out_specs=pl.BlockSpec((tm, d), lambda i, j: (i, 0)),
        out_shape=jax.ShapeDtypeStruct(x2d.shape, F32),
        scratch_shapes=[pltpu.VMEM((tm, d), BF16)],
        compiler_params=_cparams("parallel", "arbitrary", vmem_limit_bytes=FFN_VMEM_LIMIT_BYTES),
        name="ffn",
    )(*args)


def _pool_xattn_kernel(x_ref, halo_ref, g_ref, w_ref, sc_ref, gx_ref, wq_ref, k_ref, v_ref, wo_ref,
                       o_ref, hp_ref, x1_ref, *, ts):
    i = pl.program_id(1)
    x = x_ref[0]
    g = g_ref[...]
    hn = _rms(x, g)
    hh = _rms(halo_ref[0], g) * jnp.where(i > 0, 1.0, 0.0)
    hp_ref[0:POOL_HALO, :] = hh
    hp_ref[POOL_HALO:, :] = hn
    pos = i * ts + lax.broadcasted_iota(jnp.int32, (ts, 1), 0)
    gc = x.shape[-1] // len(POOL_WINDOWS)
    for gi, win in enumerate(POOL_WINDOWS):
        cols = slice(gi * gc, (gi + 1) * gc)
        acc = hp_ref[:, cols]
        step = 1
        while step < win:
            acc = acc + pltpu.roll(acc, step, axis=0)
            step *= 2
        cnt = jnp.minimum(pos + 1, win).astype(F32)
        mean = acc[POOL_HALO:, :] / cnt
        diff = (mean - hn[:, cols]).astype(BF16)
        y = _dot(diff, w_ref[gi])
        x1_ref[:, cols] = x[:, cols] + y * sc_ref[:, cols]
    o_ref[0] = _xattn_apply(x1_ref[...], gx_ref, wq_ref, k_ref, v_ref, wo_ref)


def _pool_xattn(x, layer, g, pool_layer, w, sc, xa):
    b, s, d = x.shape
    ts = min(SEQ_ROWS, s)
    hb = ts // POOL_HALO
    _, groups, gc, _ = w.shape
    xa_specs, xa_args = _xattn_operands(xa, d)
    return pl.pallas_call(
        functools.partial(_pool_xattn_kernel, ts=ts),
        grid=(b, s // ts),
        in_specs=[
            pl.BlockSpec((1, ts, d), lambda bi, i: (bi, i, 0)),
            pl.BlockSpec((1, POOL_HALO, d), lambda bi, i: (bi, jnp.maximum(i * hb - 1, 0), 0)),
            _const_spec((1, d), layer),
            _const_spec((groups, gc, gc), pool_layer),
            _const_spec((1, d), pool_layer),
        ] + xa_specs,
        out_specs=pl.BlockSpec((1, ts, d), lambda bi, i: (bi, i, 0)),
        out_shape=jax.ShapeDtypeStruct(x.shape, F32),
        scratch_shapes=[pltpu.VMEM((ts + POOL_HALO, d), F32), pltpu.VMEM((ts, d), F32)],
        compiler_params=_cparams("parallel", "arbitrary"),
        name="pool_xattn",
    )(x, x, g, w, sc, *xa_args)


def _dsa_proj_kernel(x_ref, g_ref, wq_ref, wkT_ref, wsm_ref, wqiT_ref, wwiT_ref,
                     cq_ref, sq_ref, caT_ref, saT_ref, ciT_ref, siT_ref, cci_ref, ssi_ref,
                     q_ref, kT_ref, v_ref, qiT_ref, ki_ref, wiT_ref, *, n_heads):
    hn = _rms(x_ref[0], g_ref[...]).astype(BF16)

    q = _dot(hn, wq_ref[...])
    cq = cq_ref[0]
    sq = sq_ref[0]
    for h in range(n_heads):
        qh = q[:, h * HEAD_DIM:(h + 1) * HEAD_DIM]
        q_ref[0, h] = (qh * cq + pltpu.roll(qh, HEAD_DIM // 2, axis=1) * sq).astype(BF16)

    kT = _dot_nt(wkT_ref[...], hn)
    half = HEAD_DIM // 2
    k1, k2 = kT[:half], kT[half:]
    ca, sa = caT_ref[0], saT_ref[0]
    kT_ref[0, :half, :] = (k1 * ca - k2 * sa).astype(BF16)
    kT_ref[0, half:, :] = (k2 * ca + k1 * sa).astype(BF16)

    sm = _dot(hn, wsm_ref[...])
    v_ref[0, :, :HEAD_DIM] = sm[:, :HEAD_DIM].astype(BF16)
    v_ref[0, :, HEAD_DIM:] = jnp.ones((sm.shape[0], HEAD_DIM), BF16)
    ki = sm[:, HEAD_DIM:HEAD_DIM + IDX_DIM] * cci_ref[0] + sm[:, HEAD_DIM + IDX_DIM:] * ssi_ref[0]
    ki_ref[0] = ki.astype(BF16)

    qiT = _dot_nt(wqiT_ref[...], hn)
    ci, si = ciT_ref[0], siT_ref[0]
    ih = IDX_DIM // 2
    for h in range(IDX_HEADS):
        r0 = h * IDX_DIM
        x1, x2 = qiT[r0:r0 + ih], qiT[r0 + ih:r0 + IDX_DIM]
        qiT_ref[0, r0:r0 + ih, :] = (x1 * ci - x2 * si).astype(BF16)
        qiT_ref[0, r0 + ih:r0 + IDX_DIM, :] = (x2 * ci + x1 * si).astype(BF16)

    wiT_ref[0] = _dot_nt(wwiT_ref[...], hn) * (IDX_HEADS ** -0.5 * IDX_DIM ** -0.5)


def _dsa_proj(x, layer, g, w, tabs):
    b, s, d = x.shape
    n_heads = d // HEAD_DIM
    ts = min(SEQ_ROWS, s)
    qi_rows = IDX_HEADS * IDX_DIM
    in_specs = [
        _seq_tile((1, ts, d), 1),
        _const_spec((1, d), layer),
        _const_spec(w["wq"].shape), _const_spec(w["wkT"].shape), _const_spec(w["wsm"].shape),
        _const_spec(w["wqiT"].shape), _const_spec(w["wwiT"].shape),
        _seq_tile((1, ts, HEAD_DIM), 1), _seq_tile((1, ts, HEAD_DIM), 1),
        _seq_tile((1, HEAD_DIM // 2, ts), 2), _seq_tile((1, HEAD_DIM // 2, ts), 2),
        _seq_tile((1, IDX_DIM // 2, ts), 2), _seq_tile((1, IDX_DIM // 2, ts), 2),
        _seq_tile((1, ts, IDX_DIM), 1), _seq_tile((1, ts, IDX_DIM), 1),
    ]
    out_specs = [
        _seq_tile((1, n_heads, ts, HEAD_DIM), 2),
        _seq_tile((1, HEAD_DIM, ts), 2),
        _seq_tile((1, ts, 2 * HEAD_DIM), 1),
        _seq_tile((1, qi_rows, ts), 2),
        _seq_tile((1, ts, IDX_DIM), 1),
        _seq_tile((1, IDX_HEADS, ts), 2),
    ]
    out_shape = [
        jax.ShapeDtypeStruct((b, n_heads, s, HEAD_DIM), BF16),
        jax.ShapeDtypeStruct((b, HEAD_DIM, s), BF16),
        jax.ShapeDtypeStruct((b, s, 2 * HEAD_DIM), BF16),
        jax.ShapeDtypeStruct((b, qi_rows, s), BF16),
        jax.ShapeDtypeStruct((b, s, IDX_DIM), BF16),
        jax.ShapeDtypeStruct((b, IDX_HEADS, s), F32),
    ]
    return pl.pallas_call(
        functools.partial(_dsa_proj_kernel, n_heads=n_heads),
        grid=(b, s // ts),
        in_specs=in_specs,
        out_specs=out_specs,
        out_shape=out_shape,
        compiler_params=_cparams("parallel", "arbitrary"),
        name="dsa_proj",
    )(x, g, w["wq"], w["wkT"], w["wsm"], w["wqiT"], w["wwiT"],
      tabs["cq"], tabs["sq"], tabs["caT"], tabs["saT"], tabs["ciT"], tabs["siT"], tabs["cci"], tabs["ssi"])


SCORE_ROWS = 128
SCORE_UNROLL = 8
TIE_ROWS = 256
PLANE_ROWS = 256
HEAD_GROUP = 4
ATT_ROWS = 32


def _bit_transpose32(words):
    words = list(words)
    j, m = 16, 0x0000FFFF
    while j:
        mask = jnp.int32(m)
        for k in range(32):
            if not k & j:
                t = (words[k] ^ (words[k + j] >> j)) & mask
                words[k] = words[k] ^ t
                words[k + j] = words[k + j] ^ (t << j)
        j >>= 1
        m = (m ^ (m << j)) & 0xFFFFFFFF
    return words


def _dsa_attn_kernel(qiT_ref, ki_ref, wiT_ref, q_ref, kT_ref, v_ref, prev_ref, o_ref,
                     keys_ref, planes_ref, biasT_ref, bias_ref, *, s_c, qn, topk, qb0, n_heads):
    del prev_ref
    q0 = (pl.program_id(1) + qb0) * qn
    t_pos = q0 + lax.broadcasted_iota(jnp.int32, (1, qn), 1)
    i_zero, i_one, i_min = jnp.int32(0), jnp.int32(1), jnp.int32(INT_MIN)
    f_zero, f_one, f_mask = jnp.float32(0.0), jnp.float32(1.0), jnp.float32(MASK_BIAS)

    def score_chunk(c, carry):
        r0 = pl.multiple_of(c * SCORE_ROWS, SCORE_ROWS)
        kic = ki_ref[0, pl.ds(r0, SCORE_ROWS), :]
        acc = jnp.zeros((SCORE_ROWS, qn), F32)
        for h in range(IDX_HEADS):
            rel = _dot(kic, qiT_ref[0, h * IDX_DIM:(h + 1) * IDX_DIM, :])
            acc = acc + wiT_ref[0, h:h + 1, :] * jnp.maximum(rel, 0.0)
        bits = pltpu.bitcast(acc + 0.0, jnp.int32)
        key = bits ^ ((bits >> 31) & jnp.int32(0x7FFFFFFF))
        s_pos = r0 + lax.broadcasted_iota(jnp.int32, (SCORE_ROWS, 1), 0)
        keys_ref[pl.ds(r0, SCORE_ROWS), :] = jnp.where(s_pos <= t_pos, key, i_min)
        return carry

    n_chunks = s_c // SCORE_ROWS
    unroll = max(u for u in range(1, SCORE_UNROLL + 1) if n_chunks % u == 0)
    lax.fori_loop(0, n_chunks, score_chunk, 0, unroll=unroll)

    def plane_block(bi, carry):
        r0 = bi * PLANE_ROWS
        p0 = pl.multiple_of(bi * 8, 8)
        for lt in range(qn // 128):
            lanes = slice(lt * 128, (lt + 1) * 128)
            words = [keys_ref[pl.ds(pl.multiple_of(r0 + 8 * j, 8), 8), lanes] ^ i_min for j in range(32)]
            for b, plane in enumerate(_bit_transpose32(words)):
                planes_ref[b, pl.ds(p0, 8), lanes] = plane
        return carry

    lax.fori_loop(0, s_c // PLANE_ROWS, plane_block, 0)

    kk = jnp.minimum(t_pos + 1, topk)

    def popcount_rows(words):
        return jnp.sum(lax.population_count(words), axis=0, keepdims=True)

    def refine(b, carry):
        alive, need, ans = carry
        ones = alive & planes_ref[b]
        cnt = popcount_rows(ones)
        take = cnt >= need
        alive = jnp.where(take, ones, alive ^ ones)
        need = jnp.where(take, need, need - cnt)
        ans = jnp.where(take, ans | jnp.left_shift(i_one, 31 - b), ans)
        return alive, need, ans

    alive = jnp.full((s_c // 32, qn), -1, jnp.int32)
    alive, need, ans = lax.fori_loop(0, 32, refine, (alive, kk, jnp.zeros((1, qn), jnp.int32)))
    ans = ans ^ i_min

    biasT_ref[...] = jnp.where(keys_ref[...] >= ans, f_zero, f_mask)
    tie = jnp.max(jnp.where(popcount_rows(alive) > need, i_one, i_zero))

    @pl.when(tie > 0)
    def _():
        needf = need.astype(F32)
        rr = lax.broadcasted_iota(jnp.int32, (TIE_ROWS, TIE_ROWS), 0)
        cc = lax.broadcasted_iota(jnp.int32, (TIE_ROWS, TIE_ROWS), 1)
        tri = jnp.where(cc < rr, f_one, f_zero).astype(BF16)
        carry = jnp.zeros((1, qn), F32)
        for c in range(s_c // TIE_ROWS):
            rows = slice(c * TIE_ROWS, (c + 1) * TIE_ROWS)
            kc = keys_ref[rows, :]
            eq = jnp.where(kc == ans, f_one, f_zero)
            before = _dot(tri, eq.astype(BF16)) + carry
            keep = jnp.where(kc > ans, f_one, eq * jnp.where(before < needf, f_one, f_zero))
            biasT_ref[rows, :] = jnp.where(keep > 0.5, f_zero, f_mask)
            carry = carry + jnp.sum(eq, axis=0, keepdims=True)

    for c in range(s_c // qn):
        cols = slice(c * qn, (c + 1) * qn)
        bias_ref[:, cols] = biasT_ref[cols, :].T

    kT = kT_ref[0]
    v1 = v_ref[0]
    hg = HEAD_GROUP
    tiles = [(g0, r0) for g0 in range(0, n_heads, hg) for r0 in range(0, qn, ATT_ROWS)]

    def qk(tile):
        g0, r0 = tile
        qh = q_ref[0, g0:g0 + hg, r0:r0 + ATT_ROWS, :].reshape(hg * ATT_ROWS, HEAD_DIM)
        return _dot(qh, kT).reshape(hg, ATT_ROWS, s_c) + bias_ref[r0:r0 + ATT_ROWS, :][None]

    logits_next = qk(tiles[0])
    for ti, (g0, r0) in enumerate(tiles):
        logits = logits_next
        if ti + 1 < len(tiles):
            logits_next = qk(tiles[ti + 1])
        m = jnp.max(logits, axis=-1, keepdims=True)
        e = jnp.exp2(logits - m)
        o = _dot(e.astype(BF16).reshape(hg * ATT_ROWS, s_c), v1).reshape(hg, ATT_ROWS, 2 * HEAD_DIM)
        o = o[:, :, :HEAD_DIM] / o[:, :, HEAD_DIM:]
        for j in range(hg):
            o_ref[0, r0:r0 + ATT_ROWS, (g0 + j) * HEAD_DIM:(g0 + j + 1) * HEAD_DIM] = o[j].astype(BF16)


def _dsa_attn(p, prev, *, qb0, nqb, qn, topk):
    q, kT, v, qiT, ki, wiT = p
    b, n_heads, s, _ = q.shape
    s_c = (qb0 + nqb) * qn
    d = n_heads * HEAD_DIM
    qi_rows = IDX_HEADS * IDX_DIM
    kern = functools.partial(_dsa_attn_kernel, s_c=s_c, qn=qn, topk=topk, qb0=qb0, n_heads=n_heads)
    in_specs = [
        pl.BlockSpec((1, qi_rows, qn), lambda bi, i: (bi, 0, qb0 + i)),
        pl.BlockSpec((1, s_c, IDX_DIM), lambda bi, i: (bi, 0, 0)),
        pl.BlockSpec((1, IDX_HEADS, qn), lambda bi, i: (bi, 0, qb0 + i)),
        pl.BlockSpec((1, n_heads, qn, HEAD_DIM), lambda bi, i: (bi, 0, qb0 + i, 0)),
        pl.BlockSpec((1, HEAD_DIM, s_c), lambda bi, i: (bi, 0, 0)),
        pl.BlockSpec((1, s_c, 2 * HEAD_DIM), lambda bi, i: (bi, 0, 0)),
        pl.BlockSpec(memory_space=pl.ANY),
    ]
    return pl.pallas_call(
        kern,
        grid=(b, nqb),
        in_specs=in_specs,
        out_specs=pl.BlockSpec((1, qn, d), lambda bi, i: (bi, qb0 + i, 0)),
        out_shape=jax.ShapeDtypeStruct((b, s, d), BF16),
        scratch_shapes=[
            pltpu.VMEM((s_c, qn), jnp.int32),
            pltpu.VMEM((32, s_c // 32, qn), jnp.int32),
            pltpu.VMEM((s_c, qn), F32),
            pltpu.VMEM((qn, s_c), F32),
        ],
        input_output_aliases={6: 0},
        compiler_params=_cparams("parallel", "arbitrary"),
        name=f"dsa_attn_{s_c}",
    )(qiT, ki, wiT, q, kT, v, prev)


def _proj_xattn_kernel(x_ref, a_ref, w_ref, gx_ref, wq_ref, k_ref, v_ref, wo_ref, o_ref):
    x1 = x_ref[0] + _dot(a_ref[0], w_ref[...])
    o_ref[0] = _xattn_apply(x1, gx_ref, wq_ref, k_ref, v_ref, wo_ref)


def _proj_xattn(x, a, attn_layer, w, xa):
    b, s, d = x.shape
    k = a.shape[-1]
    ts = min(SEQ_ROWS, s)
    xa_specs, xa_args = _xattn_operands(xa, d)
    return pl.pallas_call(
        _proj_xattn_kernel,
        grid=(b, s // ts),
        in_specs=[
            pl.BlockSpec((1, ts, d), lambda bi, i: (bi, i, 0)),
            pl.BlockSpec((1, ts, k), lambda bi, i: (bi, i, 0)),
            _const_spec((k, d), attn_layer),
        ] + xa_specs,
        out_specs=pl.BlockSpec((1, ts, d), lambda bi, i: (bi, i, 0)),
        out_shape=jax.ShapeDtypeStruct(x.shape, F32),
        compiler_params=_cparams("parallel", "arbitrary"),
        name="proj_xattn",
    )(x, a, w, *xa_args)


def _rope_tables(positions):
    def cs(dim):
        inv_freq = ROPE_THETA ** (-(jnp.arange(0, dim, 2, dtype=F32) / dim))
        ang = positions.astype(F32)[..., None] * inv_freq
        return jnp.cos(ang), jnp.sin(ang)

    cos_a, sin_a, cos_i, sin_i = lax.optimization_barrier(cs(HEAD_DIM) + cs(IDX_DIM))
    tr = lambda t: jnp.swapaxes(t, 1, 2)
    qs = HEAD_DIM ** -0.5 * math.log2(math.e)
    return {
        "cq": qs * jnp.concatenate([cos_a, cos_a], -1), "sq": qs * jnp.concatenate([-sin_a, sin_a], -1),
        "caT": tr(cos_a), "saT": tr(sin_a),
        "ciT": tr(cos_i), "siT": tr(sin_i),
        "cci": jnp.concatenate([cos_i, cos_i], -1), "ssi": jnp.concatenate([sin_i, sin_i], -1),
    }


def _dsa_weights(w_in, d):
    n_q = d
    o = 0
    wq = w_in[:, o:o + n_q]; o += n_q
    wk = w_in[:, o:o + HEAD_DIM]; o += HEAD_DIM
    wv = w_in[:, o:o + HEAD_DIM]; o += HEAD_DIM
    wqi = w_in[:, o:o + IDX_HEADS * IDX_DIM]; o += IDX_HEADS * IDX_DIM
    wki = w_in[:, o:o + IDX_DIM]; o += IDX_DIM
    wwi = w_in[:, o:o + IDX_HEADS]
    ih = IDX_DIM // 2
    wsm = jnp.concatenate([wv, wki, -wki[:, ih:], wki[:, :ih]], axis=1)
    c = lambda t: t.astype(BF16)
    return {"wq": c(wq), "wkT": c(wk.T), "wsm": c(wsm), "wqiT": c(wqi.T), "wwiT": c(wwi.T)}


def kernel(x, mem, positions, norm_mix, norm_xattn, norm_ffn, norm_memory, norm_final, pool_w, pool_scale,
           attn_w_in, attn_w_out, xattn_w_q, xattn_w_kv, xattn_w_o, ffn_w_in, ffn_w_out):
    b, s, d = x.shape
    depth = norm_mix.shape[0]
    m = mem.shape[1]
    rows = lambda t: t.reshape(-1, 1, d)
    c = lambda t: t.astype(BF16)

    tabs = _rope_tables(positions)
    kv_all = _memkv(mem.reshape(b * m, d), norm_memory.reshape(1, d), c(xattn_w_kv)).reshape(depth, b, m, -1)
    g_mix, g_xattn, g_ffn = rows(norm_mix), rows(norm_xattn), rows(norm_ffn)
    w_pool, sc_pool = c(pool_w), rows(pool_scale)
    w_attn_out = c(attn_w_out)
    w_xq, w_xo = c(xattn_w_q), c(xattn_w_o)
    w_ffn_in = c(ffn_w_in)

    topk = min(TOPK_MAX, s // 4)
    qn = min(256, s)

    a = jnp.zeros((b, s, d), BF16)
    ia = ib = 0
    for i in range(depth):
        xa = (i, g_xattn, w_xq, kv_all, w_xo)
        if i % 2 == 0:
            x = _pool_xattn(x, i, g_mix, ia, w_pool, sc_pool, xa)
            ia += 1
        else:
            p = _dsa_proj(x, i, g_mix, _dsa_weights(attn_w_in[ib], d), tabs)
            for qb0 in range(s // qn):
                a = _dsa_attn(p, a, qb0=qb0, nqb=1, qn=qn, topk=topk)
            x = _proj_xattn(x, a, ib, w_attn_out, xa)
            ib += 1
        g_out = norm_final.reshape(1, d) if i == depth - 1 else None
        x = _ffn(x.reshape(b * s, d), i, g_ffn, w_ffn_in, ffn_w_out, g_out).reshape(b, s, d)
    return x
```

```python
import functools
import math

import jax
import jax.numpy as jnp
from jax import lax
from jax.experimental import pallas as pl
from jax.experimental.pallas import tpu as pltpu

EPS = 1e-6
ROPE_THETA = 10000.0
POOL_WINDOWS = (2, 4, 8, 16)
POOL_HALO = 16
HEAD_DIM = 128
IDX_HEADS = 16
IDX_DIM = 64
TOPK_MAX = 256
X_HEADS = 4
X_HEAD_DIM = 128

BF16 = jnp.bfloat16
F32 = jnp.float32
INT_MIN = -(2**31)
MASK_BIAS = -1e30
VMEM_LIMIT_BYTES = 56 * 1024 * 1024
FFN_VMEM_LIMIT_BYTES = 60 * 1024 * 1024
SEQ_ROWS = 512
FFN_ROWS = 1024
FFN_COLS = 512

_NT = (((1,), (1,)), ((), ()))


def _cparams(*sem, vmem_limit_bytes=VMEM_LIMIT_BYTES):
    return pltpu.CompilerParams(dimension_semantics=sem, vmem_limit_bytes=vmem_limit_bytes)


def _rms(x, g):
    ms = jnp.mean(x * x, axis=-1, keepdims=True)
    return x * lax.rsqrt(ms + EPS) * g


def _dot(a, b):
    return jnp.dot(a, b, preferred_element_type=F32)


def _dot_nt(a, b):
    return lax.dot_general(a, b, _NT, preferred_element_type=F32)


def _const_spec(shape, layer=None):
    n = len(shape)
    if layer is None:
        return pl.BlockSpec(shape, lambda *_: (0,) * n, pipeline_mode=pl.Buffered(1))
    return pl.BlockSpec((None,) + tuple(shape), lambda *_: (layer,) + (0,) * n, pipeline_mode=pl.Buffered(1))


def _seq_tile(shape, seq_axis):
    def imap(bi, i):
        idx = [0] * len(shape)
        idx[0] = bi
        idx[seq_axis] = i
        return tuple(idx)
    return pl.BlockSpec(shape, imap)


def _memkv_kernel(mem_ref, g_ref, w_ref, o_ref):
    hn = _rms(mem_ref[...], g_ref[...]).astype(BF16)
    o_ref[...] = _dot(hn, w_ref[...]).astype(BF16)


def _memkv(mem2d, g, w_kv):
    rows, d = mem2d.shape
    depth, _, n = w_kv.shape
    tr = min(512, rows)
    return pl.pallas_call(
        _memkv_kernel,
        grid=(rows // tr, depth),
        in_specs=[
            pl.BlockSpec((tr, d), lambda i, l: (i, 0)),
            pl.BlockSpec((1, d), lambda i, l: (0, 0)),
            pl.BlockSpec((None, d, n), lambda i, l: (l, 0, 0)),
        ],
        out_specs=pl.BlockSpec((None, tr, n), lambda i, l: (l, i, 0)),
        out_shape=jax.ShapeDtypeStruct((depth, rows, n), BF16),
        compiler_params=_cparams("parallel", "arbitrary"),
        name="memkv",
    )(mem2d, g, w_kv)


def _xattn_apply(x, g_ref, wq_ref, k_ref, v_ref, wo_ref):
    hn = _rms(x, g_ref[...]).astype(BF16)
    q = _dot(hn, wq_ref[...])
    exp2_scale = X_HEAD_DIM ** -0.5 * math.log2(math.e)
    outs = []
    for h in range(X_HEADS):
        sl = slice(h * X_HEAD_DIM, (h + 1) * X_HEAD_DIM)
        logits = _dot_nt(q[:, sl].astype(BF16), k_ref[0, :, sl])
        m = jnp.max(logits, axis=-1, keepdims=True)
        e = jnp.exp2((logits - m) * exp2_scale)
        s = jnp.sum(e, axis=-1, keepdims=True)
        oh = _dot(e.astype(BF16), v_ref[0, :, sl]) / s
        outs.append(oh.astype(BF16))
    a = jnp.concatenate(outs, axis=-1)
    return x + _dot(a, wo_ref[...])


def _xattn_operands(xa, d):
    layer, g, wq, kv, wo = xa
    m = kv.shape[2]
    inner = X_HEADS * X_HEAD_DIM
    specs = [
        _const_spec((1, d), layer),
        _const_spec((d, inner), layer),
        pl.BlockSpec((None, 1, m, inner), lambda bi, i: (layer, bi, 0, 0)),
        pl.BlockSpec((None, 1, m, inner), lambda bi, i: (layer, bi, 0, 1)),
        _const_spec((inner, d), layer),
    ]
    return specs, [g, wq, kv, kv, wo]


def _ffn_kernel(x_ref, g_ref, wg_ref, wu_ref, wo_ref, *rest):
    o_ref, hn_ref = rest[-2:]
    j = pl.program_id(1)

    @pl.when(j == 0)
    def _():
        x = x_ref[...]
        hn_ref[...] = _rms(x, g_ref[...]).astype(BF16)
        o_ref[...] = x

    hn = hn_ref[...]
    gate = _dot(hn, wg_ref[...])
    up = _dot(hn, wu_ref[...])
    act = (gate / (1.0 + jnp.exp(-gate)) * up).astype(BF16)
    o_ref[...] += _dot(act, wo_ref[...].astype(BF16))

    if len(rest) == 3:
        @pl.when(j == pl.num_programs(1) - 1)
        def _():
            o_ref[...] = _rms(o_ref[...], rest[0][...])


def _ffn(x2d, layer, g, w_in, w_out, g_out=None):
    rows, d = x2d.shape
    hidden = w_out.shape[1]
    tm = min(FFN_ROWS, rows)
    th = FFN_COLS if hidden % FFN_COLS == 0 else 256
    nh = hidden // th
    in_specs = [
        pl.BlockSpec((tm, d), lambda i, j: (i, 0)),
        _const_spec((1, d), layer),
        pl.BlockSpec((None, d, th), lambda i, j: (layer, 0, j)),
        pl.BlockSpec((None, d, th), lambda i, j: (layer, 0, nh + j)),
        pl.BlockSpec((None, th, d), lambda i, j: (layer, j, 0)),
    ]
    args = [x2d, g, w_in, w_in, w_out]
    if g_out is not None:
        in_specs.append(_const_spec((1, d)))
        args.append(g_out)
    return pl.pallas_call(
        _ffn_kernel,
        grid=(rows // tm, nh),
        in_specs=in_specs,
        out_specs=pl.BlockSpec((tm, d), lambda i, j: (i, 0)),
        out_shape=jax.ShapeDtypeStruct(x2d.shape, F32),
        scratch_shapes=[pltpu.VMEM((tm, d), BF16)],
        compiler_params=_cparams("parallel", "arbitrary", vmem_limit_bytes=FFN_VMEM_LIMIT_BYTES),
        name="ffn",
    )(*args)


def _pool_xattn_kernel(x_ref, halo_ref, g_ref, w_ref, sc_ref, gx_ref, wq_ref, k_ref, v_ref, wo_ref,
                       o_ref, hp_ref, x1_ref, *, ts):
    i = pl.program_id(1)
    x = x_ref[0]
    g = g_ref[...]
    hn = _rms(x, g)
    hh = _rms(halo_ref[0], g) * jnp.where(i > 0, 1.0, 0.0)
    hp_ref[0:POOL_HALO, :] = hh
    hp_ref[POOL_HALO:, :] = hn
    pos = i * ts + lax.broadcasted_iota(jnp.int32, (ts, 1), 0)
    gc = x.shape[-1] // len(POOL_WINDOWS)
    for gi, win in enumerate(POOL_WINDOWS):
        cols = slice(gi * gc, (gi + 1) * gc)
        acc = hp_ref[:, cols]
        step = 1
        while step < win:
            acc = acc + pltpu.roll(acc, step, axis=0)
            step *= 2
        cnt = jnp.minimum(pos + 1, win).astype(F32)
        mean = acc[POOL_HALO:, :] / cnt
        diff = (mean - hn[:, cols]).astype(BF16)
        y = _dot(diff, w_ref[gi])
        x1_ref[:, cols] = x[:, cols] + y * sc_ref[:, cols]
    o_ref[0] = _xattn_apply(x1_ref[...], gx_ref, wq_ref, k_ref, v_ref, wo_ref)


def _pool_xattn(x, layer, g, pool_layer, w, sc, xa):
    b, s, d = x.shape
    ts = min(SEQ_ROWS, s)
    hb = ts // POOL_HALO
    _, groups, gc, _ = w.shape
    xa_specs, xa_args = _xattn_operands(xa, d)
    return pl.pallas_call(
        functools.partial(_pool_xattn_kernel, ts=ts),
        grid=(b, s // ts),
        in_specs=[
            pl.BlockSpec((1, ts, d), lambda bi, i: (bi, i, 0)),
            pl.BlockSpec((1, POOL_HALO, d), lambda bi, i: (bi, jnp.maximum(i * hb - 1, 0), 0)),
            _const_spec((1, d), layer),
            _const_spec((groups, gc, gc), pool_layer),
            _const_spec((1, d), pool_layer),
        ] + xa_specs,
        out_specs=pl.BlockSpec((1, ts, d), lambda bi, i: (bi, i, 0)),
        out_shape=jax.ShapeDtypeStruct(x.shape, F32),
        scratch_shapes=[pltpu.VMEM((ts + POOL_HALO, d), F32), pltpu.VMEM((ts, d), F32)],
        compiler_params=_cparams("parallel", "arbitrary"),
        name="pool_xattn",
    )(x, x, g, w, sc, *xa_args)


def _dsa_proj_kernel(x_ref, g_ref, wq_ref, wkT_ref, wsm_ref, wqiT_ref, wwiT_ref,
                     cq_ref, sq_ref, caT_ref, saT_ref, ciT_ref, siT_ref, cci_ref, ssi_ref,
                     q_ref, kT_ref, v_ref, qiT_ref, ki_ref, wiT_ref, *, n_heads):
    hn = _rms(x_ref[0], g_ref[...]).astype(BF16)

    q = _dot(hn, wq_ref[...])
    cq = cq_ref[0]
    sq = sq_ref[0]
    for h in range(n_heads):
        qh = q[:, h * HEAD_DIM:(h + 1) * HEAD_DIM]
        q_ref[0, h] = (qh * cq + pltpu.roll(qh, HEAD_DIM // 2, axis=1) * sq).astype(BF16)

    kT = _dot_nt(wkT_ref[...], hn)
    half = HEAD_DIM // 2
    k1, k2 = kT[:half], kT[half:]
    ca, sa = caT_ref[0], saT_ref[0]
    kT_ref[0, :half, :] = (k1 * ca - k2 * sa).astype(BF16)
    kT_ref[0, half:, :] = (k2 * ca + k1 * sa).astype(BF16)

    sm = _dot(hn, wsm_ref[...])
    v_ref[0, :, :HEAD_DIM] = sm[:, :HEAD_DIM].astype(BF16)
    v_ref[0, :, HEAD_DIM:] = jnp.ones((sm.shape[0], HEAD_DIM), BF16)
    ki = sm[:, HEAD_DIM:HEAD_DIM + IDX_DIM] * cci_ref[0] + sm[:, HEAD_DIM + IDX_DIM:] * ssi_ref[0]
    ki_ref[0] = ki.astype(BF16)

    qiT = _dot_nt(wqiT_ref[...], hn)
    ci, si = ciT_ref[0], siT_ref[0]
    ih = IDX_DIM // 2
    for h in range(IDX_HEADS):
        r0 = h * IDX_DIM
        x1, x2 = qiT[r0:r0 + ih], qiT[r0 + ih:r0 + IDX_DIM]
        qiT_ref[0, r0:r0 + ih, :] = (x1 * ci - x2 * si).astype(BF16)
        qiT_ref[0, r0 + ih:r0 + IDX_DIM, :] = (x2 * ci + x1 * si).astype(BF16)

    wiT_ref[0] = _dot_nt(wwiT_ref[...], hn) * (IDX_HEADS ** -0.5 * IDX_DIM ** -0.5)


def _dsa_proj(x, layer, g, w, tabs):
    b, s, d = x.shape
    n_heads = d // HEAD_DIM
    ts = min(SEQ_ROWS, s)
    qi_rows = IDX_HEADS * IDX_DIM
    in_specs = [
        _seq_tile((1, ts, d), 1),
        _const_spec((1, d), layer),
        _const_spec(w["wq"].shape), _const_spec(w["wkT"].shape), _const_spec(w["wsm"].shape),
        _const_spec(w["wqiT"].shape), _const_spec(w["wwiT"].shape),
        _seq_tile((1, ts, HEAD_DIM), 1), _seq_tile((1, ts, HEAD_DIM), 1),
        _seq_tile((1, HEAD_DIM // 2, ts), 2), _seq_tile((1, HEAD_DIM // 2, ts), 2),
        _seq_tile((1, IDX_DIM // 2, ts), 2), _seq_tile((1, IDX_DIM // 2, ts), 2),
        _seq_tile((1, ts, IDX_DIM), 1), _seq_tile((1, ts, IDX_DIM), 1),
    ]
    out_specs = [
        _seq_tile((1, n_heads, ts, HEAD_DIM), 2),
        _seq_tile((1, HEAD_DIM, ts), 2),
        _seq_tile((1, ts, 2 * HEAD_DIM), 1),
        _seq_tile((1, qi_rows, ts), 2),
        _seq_tile((1, ts, IDX_DIM), 1),
        _seq_tile((1, IDX_HEADS, ts), 2),
    ]
    out_shape = [
        jax.ShapeDtypeStruct((b, n_heads, s, HEAD_DIM), BF16),
        jax.ShapeDtypeStruct((b, HEAD_DIM, s), BF16),
        jax.ShapeDtypeStruct((b, s, 2 * HEAD_DIM), BF16),
        jax.ShapeDtypeStruct((b, qi_rows, s), BF16),
        jax.ShapeDtypeStruct((b, s, IDX_DIM), BF16),
        jax.ShapeDtypeStruct((b, IDX_HEADS, s), F32),
    ]
    return pl.pallas_call(
        functools.partial(_dsa_proj_kernel, n_heads=n_heads),
        grid=(b, s // ts),
        in_specs=in_specs,
        out_specs=out_specs,
        out_shape=out_shape,
        compiler_params=_cparams("parallel", "arbitrary"),
        name="dsa_proj",
    )(x, g, w["wq"], w["wkT"], w["wsm"], w["wqiT"], w["wwiT"],
      tabs["cq"], tabs["sq"], tabs["caT"], tabs["saT"], tabs["ciT"], tabs["siT"], tabs["cci"], tabs["ssi"])


SCORE_ROWS = 128
SCORE_UNROLL = 8
TIE_ROWS = 256
PLANE_ROWS = 256
HEAD_GROUP = 4
ATT_ROWS = 32
ATT_KEYS = 512


def _bit_transpose32(words):
    words = list(words)
    j, m = 16, 0x0000FFFF
    while j:
        mask = jnp.int32(m)
        for k in range(32):
            if not k & j:
                t = (words[k] ^ (words[k + j] >> j)) & mask
                words[k] = words[k] ^ t
                words[k + j] = words[k + j] ^ (t << j)
        j >>= 1
        m = (m ^ (m << j)) & 0xFFFFFFFF
    return words


def _dsa_attn_kernel(qiT_ref, ki_ref, wiT_ref, q_ref, kT_ref, v_ref, prev_ref, o_ref,
                     keys_ref, planes_ref, biasT_ref, bias_ref, *, s_c, qn, topk, qb0, n_heads):
    del prev_ref
    q0 = (pl.program_id(1) + qb0) * qn
    t_pos = q0 + lax.broadcasted_iota(jnp.int32, (1, qn), 1)
    i_zero, i_one, i_min = jnp.int32(0), jnp.int32(1), jnp.int32(INT_MIN)
    f_zero, f_one, f_mask = jnp.float32(0.0), jnp.float32(1.0), jnp.float32(MASK_BIAS)

    def score_chunk(c, carry):
        r0 = pl.multiple_of(c * SCORE_ROWS, SCORE_ROWS)
        kic = ki_ref[0, pl.ds(r0, SCORE_ROWS), :]
        acc = jnp.zeros((SCORE_ROWS, qn), F32)
        for h in range(IDX_HEADS):
            rel = _dot(kic, qiT_ref[0, h * IDX_DIM:(h + 1) * IDX_DIM, :])
            acc = acc + wiT_ref[0, h:h + 1, :] * jnp.maximum(rel, 0.0)
        bits = pltpu.bitcast(acc + 0.0, jnp.int32)
        key = bits ^ ((bits >> 31) & jnp.int32(0x7FFFFFFF))
        s_pos = r0 + lax.broadcasted_iota(jnp.int32, (SCORE_ROWS, 1), 0)
        keys_ref[pl.ds(r0, SCORE_ROWS), :] = jnp.where(s_pos <= t_pos, key, i_min)
        return carry

    n_chunks = s_c // SCORE_ROWS
    unroll = max(u for u in range(1, SCORE_UNROLL + 1) if n_chunks % u == 0)
    lax.fori_loop(0, n_chunks, score_chunk, 0, unroll=unroll)

    def plane_block(bi, carry):
        r0 = bi * PLANE_ROWS
        p0 = pl.multiple_of(bi * 8, 8)
        for lt in range(qn // 128):
            lanes = slice(lt * 128, (lt + 1) * 128)
            words = [keys_ref[pl.ds(pl.multiple_of(r0 + 8 * j, 8), 8), lanes] ^ i_min for j in range(32)]
            for b, plane in enumerate(_bit_transpose32(words)):
                planes_ref[b, pl.ds(p0, 8), lanes] = plane
        return carry

    lax.fori_loop(0, s_c // PLANE_ROWS, plane_block, 0)

    kk = jnp.minimum(t_pos + 1, topk)

    def popcount_rows(words):
        return jnp.sum(lax.population_count(words), axis=0, keepdims=True)

    def refine(b, carry):
        alive, need, ans = carry
        ones = alive & planes_ref[b]
        cnt = popcount_rows(ones)
        take = cnt >= need
        alive = jnp.where(take, ones, alive ^ ones)
        need = jnp.where(take, need, need - cnt)
        ans = jnp.where(take, ans | jnp.left_shift(i_one, 31 - b), ans)
        return alive, need, ans

    alive = jnp.full((s_c // 32, qn), -1, jnp.int32)
    alive, need, ans = lax.fori_loop(0, 32, refine, (alive, kk, jnp.zeros((1, qn), jnp.int32)))
    ans = ans ^ i_min

    biasT_ref[...] = jnp.where(keys_ref[...] >= ans, f_zero, f_mask)
    tie = jnp.max(jnp.where(popcount_rows(alive) > need, i_one, i_zero))

    @pl.when(tie > 0)
    def _():
        needf = need.astype(F32)
        rr = lax.broadcasted_iota(jnp.int32, (TIE_ROWS, TIE_ROWS), 0)
        cc = lax.broadcasted_iota(jnp.int32, (TIE_ROWS, TIE_ROWS), 1)
        tri = jnp.where(cc < rr, f_one, f_zero).astype(BF16)
        carry = jnp.zeros((1, qn), F32)
        for c in range(s_c // TIE_ROWS):
            rows = slice(c * TIE_ROWS, (c + 1) * TIE_ROWS)
            kc = keys_ref[rows, :]
            eq = jnp.where(kc == ans, f_one, f_zero)
            before = _dot(tri, eq.astype(BF16)) + carry
            keep = jnp.where(kc > ans, f_one, eq * jnp.where(before < needf, f_one, f_zero))
            biasT_ref[rows, :] = jnp.where(keep > 0.5, f_zero, f_mask)
            carry = carry + jnp.sum(eq, axis=0, keepdims=True)

    for c in range(s_c // qn):
        cols = slice(c * qn, (c + 1) * qn)
        bias_ref[:, cols] = biasT_ref[cols, :].T

    hg = HEAD_GROUP
    rows = hg * ATT_ROWS
    kc = ATT_KEYS if s_c % ATT_KEYS == 0 else qn
    for g0 in range(0, n_heads, hg):
        for r0 in range(0, qn, ATT_ROWS):
            qh = q_ref[0, g0:g0 + hg, r0:r0 + ATT_ROWS, :].reshape(rows, HEAD_DIM)
            m = jnp.full((hg, ATT_ROWS, 1), 2 * MASK_BIAS, F32)
            acc = jnp.zeros((hg, ATT_ROWS, 2 * HEAD_DIM), F32)
            for c0 in range(0, s_c, kc):
                logits = (_dot(qh, kT_ref[0, :, c0:c0 + kc]).reshape(hg, ATT_ROWS, kc)
                          + bias_ref[r0:r0 + ATT_ROWS, c0:c0 + kc][None])
                m_new = jnp.maximum(m, jnp.max(logits, axis=-1, keepdims=True))
                e = jnp.exp2(logits - m_new).astype(BF16).reshape(rows, kc)
                pv = _dot(e, v_ref[0, c0:c0 + kc, :]).reshape(hg, ATT_ROWS, 2 * HEAD_DIM)
                acc = acc * jnp.exp2(m - m_new) + pv
                m = m_new
            o = acc[:, :, :HEAD_DIM] / acc[:, :, HEAD_DIM:]
            for j in range(hg):
                o_ref[0, r0:r0 + ATT_ROWS, (g0 + j) * HEAD_DIM:(g0 + j + 1) * HEAD_DIM] = o[j].astype(BF16)


def _dsa_attn(p, prev, *, qb0, nqb, qn, topk):
    q, kT, v, qiT, ki, wiT = p
    b, n_heads, s, _ = q.shape
    s_c = (qb0 + nqb) * qn
    d = n_heads * HEAD_DIM
    qi_rows = IDX_HEADS * IDX_DIM
    kern = functools.partial(_dsa_attn_kernel, s_c=s_c, qn=qn, topk=topk, qb0=qb0, n_heads=n_heads)
    in_specs = [
        pl.BlockSpec((1, qi_rows, qn), lambda bi, i: (bi, 0, qb0 + i)),
        pl.BlockSpec((1, s_c, IDX_DIM), lambda bi, i: (bi, 0, 0)),
        pl.BlockSpec((1, IDX_HEADS, qn), lambda bi, i: (bi, 0, qb0 + i)),
        pl.BlockSpec((1, n_heads, qn, HEAD_DIM), lambda bi, i: (bi, 0, qb0 + i, 0)),
        pl.BlockSpec((1, HEAD_DIM, s_c), lambda bi, i: (bi, 0, 0)),
        pl.BlockSpec((1, s_c, 2 * HEAD_DIM), lambda bi, i: (bi, 0, 0)),
        pl.BlockSpec(memory_space=pl.ANY),
    ]
    return pl.pallas_call(
        kern,
        grid=(b, nqb),
        in_specs=in_specs,
        out_specs=pl.BlockSpec((1, qn, d), lambda bi, i: (bi, qb0 + i, 0)),
        out_shape=jax.ShapeDtypeStruct((b, s, d), BF16),
        scratch_shapes=[
            pltpu.VMEM((s_c, qn), jnp.int32),
            pltpu.VMEM((32, s_c // 32, qn), jnp.int32),
            pltpu.VMEM((s_c, qn), F32),
            pltpu.VMEM((qn, s_c), F32),
        ],
        input_output_aliases={6: 0},
        compiler_params=_cparams("parallel", "arbitrary"),
        name=f"dsa_attn_{s_c}",
    )(qiT, ki, wiT, q, kT, v, prev)


def _proj_xattn_kernel(x_ref, a_ref, w_ref, gx_ref, wq_ref, k_ref, v_ref, wo_ref, o_ref):
    x1 = x_ref[0] + _dot(a_ref[0], w_ref[...])
    o_ref[0] = _xattn_apply(x1, gx_ref, wq_ref, k_ref, v_ref, wo_ref)


def _proj_xattn(x, a, attn_layer, w, xa):
    b, s, d = x.shape
    k = a.shape[-1]
    ts = min(SEQ_ROWS, s)
    xa_specs, xa_args = _xattn_operands(xa, d)
    return pl.pallas_call(
        _proj_xattn_kernel,
        grid=(b, s // ts),
        in_specs=[
            pl.BlockSpec((1, ts, d), lambda bi, i: (bi, i, 0)),
            pl.BlockSpec((1, ts, k), lambda bi, i: (bi, i, 0)),
            _const_spec((k, d), attn_layer),
        ] + xa_specs,
        out_specs=pl.BlockSpec((1, ts, d), lambda bi, i: (bi, i, 0)),
        out_shape=jax.ShapeDtypeStruct(x.shape, F32),
        compiler_params=_cparams("parallel", "arbitrary"),
        name="proj_xattn",
    )(x, a, w, *xa_args)


def _rope_tables(positions):
    def cs(dim):
        inv_freq = ROPE_THETA ** (-(jnp.arange(0, dim, 2, dtype=F32) / dim))
        ang = positions.astype(F32)[..., None] * inv_freq
        return jnp.cos(ang), jnp.sin(ang)

    cos_a, sin_a, cos_i, sin_i = lax.optimization_barrier(cs(HEAD_DIM) + cs(IDX_DIM))
    tr = lambda t: jnp.swapaxes(t, 1, 2)
    qs = HEAD_DIM ** -0.5 * math.log2(math.e)
    return {
        "cq": qs * jnp.concatenate([cos_a, cos_a], -1), "sq": qs * jnp.concatenate([-sin_a, sin_a], -1),
        "caT": tr(cos_a), "saT": tr(sin_a),
        "ciT": tr(cos_i), "siT": tr(sin_i),
        "cci": jnp.concatenate([cos_i, cos_i], -1), "ssi": jnp.concatenate([sin_i, sin_i], -1),
    }


def _dsa_weights(w_in, d):
    n_q = d
    o = 0
    wq = w_in[:, o:o + n_q]; o += n_q
    wk = w_in[:, o:o + HEAD_DIM]; o += HEAD_DIM
    wv = w_in[:, o:o + HEAD_DIM]; o += HEAD_DIM
    wqi = w_in[:, o:o + IDX_HEADS * IDX_DIM]; o += IDX_HEADS * IDX_DIM
    wki = w_in[:, o:o + IDX_DIM]; o += IDX_DIM
    wwi = w_in[:, o:o + IDX_HEADS]
    ih = IDX_DIM // 2
    wsm = jnp.concatenate([wv, wki, -wki[:, ih:], wki[:, :ih]], axis=1)
    c = lambda t: t.astype(BF16)
    return {"wq": c(wq), "wkT": c(wk.T), "wsm": c(wsm), "wqiT": c(wqi.T), "wwiT": c(wwi.T)}


def kernel(x, mem, positions, norm_mix, norm_xattn, norm_ffn, norm_memory, norm_final, pool_w, pool_scale,
           attn_w_in, attn_w_out, xattn_w_q, xattn_w_kv, xattn_w_o, ffn_w_in, ffn_w_out):
    b, s, d = x.shape
    depth = norm_mix.shape[0]
    m = mem.shape[1]
    rows = lambda t: t.reshape(-1, 1, d)
    c = lambda t: t.astype(BF16)

    tabs = _rope_tables(positions)
    kv_all = _memkv(mem.reshape(b * m, d), norm_memory.reshape(1, d), c(xattn_w_kv)).reshape(depth, b, m, -1)
    g_mix, g_xattn, g_ffn = rows(norm_mix), rows(norm_xattn), rows(norm_ffn)
    w_pool, sc_pool = c(pool_w), rows(pool_scale)
    w_attn_out = c(attn_w_out)
    w_xq, w_xo = c(xattn_w_q), c(xattn_w_o)
    w_ffn_in = c(ffn_w_in)

    topk = min(TOPK_MAX, s // 4)
    qn = min(256, s)

    a = jnp.zeros((b, s, d), BF16)
    ia = ib = 0
    for i in range(depth):
        xa = (i, g_xattn, w_xq, kv_all, w_xo)
        if i % 2 == 0:
            x = _pool_xattn(x, i, g_mix, ia, w_pool, sc_pool, xa)
            ia += 1
        else:
            p = _dsa_proj(x, i, g_mix, _dsa_weights(attn_w_in[ib], d), tabs)
            for qb0 in range(s // qn):
                a = _dsa_attn(p, a, qb0=qb0, nqb=1, qn=qn, topk=topk)
            x = _proj_xattn(x, a, ib, w_attn_out, xa)
            ib += 1
        g_out = norm_final.reshape(1, d) if i == depth - 1 else None
        x = _ffn(x.reshape(b * s, d), i, g_ffn, w_ffn_in, ffn_w_out, g_out).reshape(b, s, d)
    return x
```
